```python
import math
import jax, jax.numpy as jnp
from jax import lax
import numpy as np

D_MODEL = 4096
BATCH = 2
SEQ = 4096
DEPTH = 1
DEC_BATCH = 4
DEC_SEQ = 4096
PAST_LEN = 128

N_META = 16
EPS = 1e-6
SSD_D_INNER = D_MODEL
SSD_HEAD_DIM = 64
SSD_N_HEADS = SSD_D_INNER // SSD_HEAD_DIM
SSD_N_GROUPS = 8
SSD_HEADS_PER_GROUP = SSD_N_HEADS // SSD_N_GROUPS
SSD_D_STATE = 128
SSD_CONV = 5
SSD_CHUNK = 128
SSD_CONV_CH = SSD_D_INNER + 2 * SSD_N_GROUPS * SSD_D_STATE
SSD_FRONT_PAD = SSD_CHUNK - N_META
ATT_N_HEADS = 16
ATT_HALF_DIM = 64
ATT_QK_DIM = 2 * ATT_HALF_DIM
ATT_V_DIM = 2 * ATT_HALF_DIM
ATT_WIDTH = ATT_N_HEADS * ATT_V_DIM
ATT_Q_BLOCK = 128
IN_Z = SSD_D_INNER
IN_XBC = SSD_CONV_CH
IN_DT = 2 * SSD_N_HEADS
IN_Q = ATT_N_HEADS * ATT_QK_DIM
IN_K = ATT_N_HEADS * ATT_QK_DIM
IN_V = ATT_WIDTH
D_IN_PROJ = IN_Z + IN_XBC + IN_DT + IN_Q + IN_K + IN_V
D_FF = -(-8 * D_MODEL // (3 * 256)) * 256

kernel_name = "hybrid_ssd_diffattn_encoder"


def rms_norm(x, g):
    xf = x.astype(jnp.float32)
    y = xf * lax.rsqrt(jnp.mean(xf * xf, axis=-1, keepdims=True) + EPS)
    return (y * g.astype(jnp.float32)).astype(x.dtype)


def centred_depthwise_conv(x, w, bias):
    half = (SSD_CONV - 1) // 2
    out = lax.conv_general_dilated(
        x, w[:, None, :].astype(x.dtype), window_strides=(1,), padding=[(half, half)],
        dimension_numbers=("NWC", "WIO", "NWC"), feature_group_count=x.shape[-1])
    return out + bias.astype(x.dtype)


def ssd_chunked(x, dt, A, Bm, Cm):
    b, Lp = x.shape[:2]
    c, T, G, J = Lp // SSD_CHUNK, SSD_CHUNK, SSD_N_GROUPS, SSD_HEADS_PER_GROUP
    P, N = SSD_HEAD_DIM, SSD_D_STATE
    dtc = dt.reshape(b, c, T, G, J)
    xdt = x.astype(jnp.float32).reshape(b, c, T, G, J, P) * dtc[..., None]
    Bc = Bm.astype(jnp.float32).reshape(b, c, T, G, N)
    Cc = Cm.astype(jnp.float32).reshape(b, c, T, G, N)
    cs = jnp.cumsum(dtc * A.reshape(G, J), axis=2)
    seg = cs[:, :, :, None] - cs[:, :, None, :]
    mask = jnp.tril(jnp.ones((T, T), dtype=bool))[:, :, None, None]
    Lmat = jnp.where(mask, jnp.exp(jnp.where(mask, seg, 0.0)), 0.0)
    CB = jnp.einsum("bclgn,bcsgn->bclsg", Cc, Bc)
    y_diag = jnp.einsum("bclsgj,bcsgjp->bclgjp", CB[..., None] * Lmat, xdt)
    decay_s = jnp.exp(cs[:, :, -1:] - cs)
    states = jnp.einsum("bcsgn,bcsgjp->bcgjpn", Bc, xdt * decay_s[..., None])
    chunk_decay = jnp.exp(cs[:, :, -1])

    def step(carry, inp):
        st, dec = inp
        return carry * dec[..., None, None] + st, carry

    init = jnp.zeros((b, G, J, P, N), jnp.float32)
    _, prev = lax.scan(step, init, (jnp.moveaxis(states, 1, 0), jnp.moveaxis(chunk_decay, 1, 0)))
    prev = jnp.moveaxis(prev, 0, 1)
    y_off = jnp.einsum("bclgn,bcgjpn->bclgjp", Cc, prev) * jnp.exp(cs)[..., None]
    return (y_diag + y_off).reshape(b, Lp, SSD_N_HEADS, P)


def bidirectional_ssd(xs, Bm, Cm, dt_raw, dt_bias, a_log, d_skip):
    L = xs.shape[1]
    pad = SSD_FRONT_PAD

    def padf(t):
        return jnp.pad(t, [(0, 0), (pad, 0)] + [(0, 0)] * (t.ndim - 2))

    valid = (jnp.arange(pad + L) >= pad).astype(jnp.float32)
    xp, Bp, Cp = padf(xs), padf(Bm), padf(Cm)
    dt = jax.nn.softplus(padf(dt_raw.astype(jnp.float32)) + dt_bias.astype(jnp.float32))
    dt = dt * valid[None, :, None, None]
    A = -jnp.exp(a_log.astype(jnp.float32))

    def rev(t):
        return jnp.flip(t, axis=1)

    y_fwd = ssd_chunked(xp, dt[:, :, 0], A[0], Bp, Cp)
    y_bwd = rev(ssd_chunked(rev(xp), rev(dt[:, :, 1]), A[1], rev(Bp), rev(Cp)))
    y = (y_fwd + y_bwd)[:, pad:] + xs.astype(jnp.float32) * d_skip.astype(jnp.float32)[:, None]
    return y.astype(xs.dtype)


def diff_attention(q, k, v, layer_idx, q_norm_g, k_norm_g, lambda_q1, lambda_k1,
                   lambda_q2, lambda_k2, subln_g):
    b, L, _ = q.shape
    H, d = ATT_N_HEADS, ATT_HALF_DIM
    qh = rms_norm(q.reshape(b, L, H, 2, d), q_norm_g) * (d ** -0.5)
    kh = rms_norm(k.reshape(b, L, H, 2, d), k_norm_g)
    vf = v.reshape(b, L, H, ATT_V_DIM).astype(jnp.float32)
    lam_init = 0.8 - 0.6 * math.exp(-0.3 * layer_idx)
    lam = (jnp.exp(jnp.sum(lambda_q1.astype(jnp.float32) * lambda_k1.astype(jnp.float32)))
           - jnp.exp(jnp.sum(lambda_q2.astype(jnp.float32) * lambda_k2.astype(jnp.float32)))
           + lam_init)
    slopes = jnp.exp2(-8.0 * jnp.arange(1, H + 1, dtype=jnp.float32) / H)
    pos = jnp.arange(L, dtype=jnp.float32)

    def attend(qb, pos_q):
        s = jnp.einsum("bqhid,bkhid->bihqk", qb, kh, preferred_element_type=jnp.float32)
        s = s - slopes[:, None, None] * jnp.abs(pos_q[:, None] - pos[None, :])
        p = jax.nn.softmax(s, axis=-1)
        a = p[:, 0] - lam * p[:, 1]
        return jnp.einsum("bhqk,bkhd->bqhd", a, vf)

    o_meta = attend(qh[:, :N_META], pos[:N_META])
    nb = (L - N_META) // ATT_Q_BLOCK
    q_blocks = jnp.moveaxis(qh[:, N_META:].reshape(b, nb, ATT_Q_BLOCK, H, 2, d), 1, 0)
    pos_blocks = pos[N_META:].reshape(nb, ATT_Q_BLOCK)
    o_real = lax.map(lambda args: attend(args[0], args[1]), (q_blocks, pos_blocks))
    o_real = jnp.moveaxis(o_real, 0, 1).reshape(b, L - N_META, H, ATT_V_DIM)
    o = jnp.concatenate([o_meta, o_real], axis=1)
    o = rms_norm(o, subln_g) * (1.0 - lam_init)
    return o.reshape(b, L, ATT_WIDTH).astype(v.dtype)


def hybrid_layer(h, layer_idx, norm_mix_g, w_in, conv_w, conv_b, dt_bias, a_log, d_skip,
                 ssd_norm_g, q_norm_g, k_norm_g, lambda_q1, lambda_k1, lambda_q2, lambda_k2,
                 subln_g, w_branch_gate, b_branch_gate, w_ssd_out, w_att_out, w_o,
                 norm_ffn_g, w_gate_up, w_down):
    b, L, _ = h.shape
    hn = rms_norm(h, norm_mix_g)
    proj = hn @ w_in
    o1 = IN_Z
    o2 = o1 + IN_XBC
    o3 = o2 + IN_DT
    o4 = o3 + IN_Q
    o5 = o4 + IN_K
    z, xbc, dt_raw, q, k, v = jnp.split(proj, [o1, o2, o3, o4, o5], axis=-1)
    xbc = jax.nn.silu(centred_depthwise_conv(xbc, conv_w, conv_b))
    gn = SSD_N_GROUPS * SSD_D_STATE
    xs, Bm, Cm = jnp.split(xbc, [SSD_D_INNER, SSD_D_INNER + gn], axis=-1)
    y_ssd = bidirectional_ssd(
        xs.reshape(b, L, SSD_N_HEADS, SSD_HEAD_DIM),
        Bm.reshape(b, L, SSD_N_GROUPS, SSD_D_STATE),
        Cm.reshape(b, L, SSD_N_GROUPS, SSD_D_STATE),
        dt_raw.reshape(b, L, 2, SSD_N_HEADS), dt_bias, a_log, d_skip)
    y_ssd = y_ssd.reshape(b, L, SSD_D_INNER) * jax.nn.silu(z)
    y_ssd = rms_norm(y_ssd.reshape(b, L, SSD_N_GROUPS, -1),
                     ssd_norm_g.reshape(SSD_N_GROUPS, -1)).reshape(b, L, SSD_D_INNER)
    y_att = diff_attention(q, k, v, layer_idx, q_norm_g, k_norm_g, lambda_q1, lambda_k1,
                           lambda_q2, lambda_k2, subln_g)
    gates = jax.nn.sigmoid((hn @ w_branch_gate + b_branch_gate).astype(jnp.float32))
    gates = gates.astype(h.dtype).reshape(b, L, 2, D_MODEL)
    mixed = gates[:, :, 0] * (y_ssd @ w_ssd_out) + gates[:, :, 1] * (y_att @ w_att_out)
    h = h + mixed @ w_o
    hf = rms_norm(h, norm_ffn_g)
    gate, up = jnp.split(hf @ w_gate_up, 2, axis=-1)
    return h + (jax.nn.silu(gate) * up) @ w_down


def run_group(x, meta_tokens, layer_weights):
    b = x.shape[0]
    meta = jnp.broadcast_to(meta_tokens.astype(x.dtype)[None], (b, N_META, D_MODEL))
    h = jnp.concatenate([meta, x], axis=1)
    for l in range(DEPTH):
        h = hybrid_layer(h, l, *[w[l] for w in layer_weights])
    return h[:, N_META:]


def setup_inputs(seed: int = 0) -> dict:
    key = jax.random.key(seed)
    ks = jax.random.split(key, 26)
    f32 = jnp.float32

    def nrm(k, shape, scale):
        return jax.random.normal(k, shape, f32) * scale

    def gain(k, shape):
        return 1.0 + 0.02 * jax.random.normal(k, shape, f32)

    dt0 = jnp.exp(jax.random.uniform(ks[7], (DEPTH, 2, SSD_N_HEADS), f32,
                                     minval=math.log(1e-3), maxval=math.log(1e-1)))
    return {
        "x_prompt": nrm(ks[0], (BATCH, SEQ, D_MODEL), 1.0),
        "x_sample": nrm(ks[1], (DEC_BATCH, DEC_SEQ, D_MODEL), 1.0),
        "meta_tokens": nrm(ks[2], (N_META, D_MODEL), 1.0),
        "norm_mix_g": gain(ks[3], (DEPTH, D_MODEL)),
        "w_in": nrm(ks[4], (DEPTH, D_MODEL, D_IN_PROJ), D_MODEL ** -0.5),
        "conv_w": nrm(ks[5], (DEPTH, SSD_CONV, SSD_CONV_CH), SSD_CONV ** -0.5),
        "conv_b": nrm(ks[6], (DEPTH, SSD_CONV_CH), 0.02),
        "dt_bias": dt0 + jnp.log(-jnp.expm1(-dt0)),
        "a_log": jnp.log(jax.random.uniform(ks[8], (DEPTH, 2, SSD_N_HEADS), f32, minval=1.0, maxval=16.0)),
        "d_skip": gain(ks[9], (DEPTH, SSD_N_HEADS)),
        "ssd_norm_g": gain(ks[10], (DEPTH, SSD_D_INNER)),
        "q_norm_g": gain(ks[11], (DEPTH, ATT_HALF_DIM)),
        "k_norm_g": gain(ks[12], (DEPTH, ATT_HALF_DIM)),
        "lambda_q1": nrm(ks[13], (DEPTH, ATT_HALF_DIM), 0.1),
        "lambda_k1": nrm(ks[14], (DEPTH, ATT_HALF_DIM), 0.1),
        "lambda_q2": nrm(ks[15], (DEPTH, ATT_HALF_DIM), 0.1),
        "lambda_k2": nrm(ks[16], (DEPTH, ATT_HALF_DIM), 0.1),
        "subln_g": gain(ks[17], (DEPTH, ATT_V_DIM)),
        "w_branch_gate": nrm(ks[18], (DEPTH, D_MODEL, 2 * D_MODEL), D_MODEL ** -0.5),
        "b_branch_gate": nrm(ks[19], (DEPTH, 2 * D_MODEL), 0.02),
        "w_ssd_out": nrm(ks[20], (DEPTH, SSD_D_INNER, D_MODEL), SSD_D_INNER ** -0.5),
        "w_att_out": nrm(ks[21], (DEPTH, ATT_WIDTH, D_MODEL), ATT_WIDTH ** -0.5),
        "w_o": nrm(ks[22], (DEPTH, D_MODEL, D_MODEL), D_MODEL ** -0.5),
        "norm_ffn_g": gain(ks[23], (DEPTH, D_MODEL)),
        "w_gate_up": nrm(ks[24], (DEPTH, D_MODEL, 2 * D_FF), D_MODEL ** -0.5),
        "w_down": nrm(ks[25], (DEPTH, D_FF, D_MODEL), D_FF ** -0.5),
    }


def reference(x_prompt, x_sample, meta_tokens, norm_mix_g, w_in, conv_w, conv_b, dt_bias,
              a_log, d_skip, ssd_norm_g, q_norm_g, k_norm_g, lambda_q1, lambda_k1,
              lambda_q2, lambda_k2, subln_g, w_branch_gate, b_branch_gate, w_ssd_out,
              w_att_out, w_o, norm_ffn_g, w_gate_up, w_down):
    layer_weights = (norm_mix_g, w_in, conv_w, conv_b, dt_bias, a_log, d_skip, ssd_norm_g,
                     q_norm_g, k_norm_g, lambda_q1, lambda_k1, lambda_q2, lambda_k2, subln_g,
                     w_branch_gate, b_branch_gate, w_ssd_out, w_att_out, w_o, norm_ffn_g,
                     w_gate_up, w_down)
    y_prompt = run_group(x_prompt, meta_tokens, layer_weights)
    y_sample = run_group(x_sample, meta_tokens, layer_weights)
    return (y_prompt, y_sample)
```

```python
import functools
import math

import jax
import jax.numpy as jnp
from jax import lax
from jax.experimental import pallas as pl
from jax.experimental.pallas import tpu as pltpu

F32 = jnp.float32
BF16 = jnp.bfloat16

D_MODEL = 4096
N_META = 16
EPS = 1e-6
SSD_HEAD_DIM = 64
SSD_N_HEADS = D_MODEL // SSD_HEAD_DIM
SSD_N_GROUPS = 8
SSD_HEADS_PER_GROUP = SSD_N_HEADS // SSD_N_GROUPS
SSD_D_STATE = 128
SSD_CONV = 5
SSD_CHUNK = 128
SSD_GN = SSD_N_GROUPS * SSD_D_STATE
SSD_CONV_CH = D_MODEL + 2 * SSD_GN
SSD_GROUP_W = SSD_HEADS_PER_GROUP * SSD_HEAD_DIM
ATT_N_HEADS = 16
ATT_HALF_DIM = 64
ATT_V_DIM = 128
ATT_WIDTH = ATT_N_HEADS * ATT_V_DIM
D_FF = 11008
LANES = 128
HALO = 16
MIB = 1024 * 1024


def _params(n_axes, vmem_mib):
    return pltpu.CompilerParams(
        dimension_semantics=("arbitrary",) * n_axes,
        vmem_limit_bytes=vmem_mib * MIB)


def _rmsnorm_kernel(x_ref, g_ref, o_ref):
    x = x_ref[...]
    ms = jnp.mean(x * x, axis=-1, keepdims=True)
    o_ref[...] = (x * lax.rsqrt(ms + EPS) * g_ref[...]).astype(o_ref.dtype)


def _rmsnorm(x, g, bm):
    m, d = x.shape
    bm = min(bm, m)
    return pl.pallas_call(
        _rmsnorm_kernel,
        grid=(m // bm,),
        in_specs=[pl.BlockSpec((bm, d), lambda i: (i, 0)),
                  pl.BlockSpec((1, d), lambda i: (0, 0))],
        out_specs=pl.BlockSpec((bm, d), lambda i: (i, 0)),
        out_shape=jax.ShapeDtypeStruct((m, d), BF16),
        compiler_params=_params(1, 32),
    )(x, g.reshape(1, d))


def _mm_kernel(*refs, dots, n_lhs, n_rhs, n_tiles, n_rows, n_consts, epilogue):
    lhs = refs[:n_lhs]
    rhs = refs[n_lhs:n_lhs + n_rhs]
    p = n_lhs + n_rhs
    tiles = refs[p:p + n_tiles]
    p += n_tiles
    rows = refs[p:p + n_rows]
    p += n_rows
    consts = refs[p:p + n_consts]
    o_ref = refs[-1]
    accs = [jnp.dot(lhs[a][...], rhs[b][...], preferred_element_type=F32)
            for a, b in dots]
    out = epilogue(accs, [t[...] for t in tiles], [r[...] for r in rows],
                   [c[...] for c in consts])
    o_ref[...] = out.astype(o_ref.dtype)


def _matmul(lhs, rhs, dots, epilogue, n_out, out_dtype, bm, bn, vmem_mib,
            tiles=(), rows=(), consts=()):
    m = lhs[0].shape[0]
    bm = min(bm, m)
    in_specs = []
    args = []
    for a in lhs:
        in_specs.append(pl.BlockSpec((bm, a.shape[1]), lambda i, j: (i, 0)))
        args.append(a)
    for w, off in rhs:
        in_specs.append(pl.BlockSpec((w.shape[0], bn), lambda i, j, off=off: (0, j + off)))
        args.append(w)
    for t, off in tiles:
        in_specs.append(pl.BlockSpec((bm, bn), lambda i, j, off=off: (i, j + off)))
        args.append(t)
    for r, off in rows:
        in_specs.append(pl.BlockSpec((1, bn), lambda i, j, off=off: (0, j + off)))
        args.append(r)
    for c in consts:
        in_specs.append(pl.BlockSpec(c.shape, lambda i, j, nd=c.ndim: (0,) * nd))
        args.append(c)
    kern = functools.partial(
        _mm_kernel, dots=tuple(dots), n_lhs=len(lhs), n_rhs=len(rhs),
        n_tiles=len(tiles), n_rows=len(rows), n_consts=len(consts), epilogue=epilogue)
    return pl.pallas_call(
        kern,
        grid=(m // bm, n_out // bn),
        in_specs=in_specs,
        out_specs=pl.BlockSpec((bm, bn), lambda i, j: (i, j)),
        out_shape=jax.ShapeDtypeStruct((m, n_out), out_dtype),
        compiler_params=_params(2, vmem_mib),
    )(*args)


def _ep_plain(accs, tiles, rows, consts):
    return accs[0]


def _ep_qknorm(accs, tiles, rows, consts):
    acc = accs[0]
    ones_bd = consts[0]
    width = ones_bd.shape[0]
    outs = []
    for c in range(acc.shape[1] // width):
        a = acc[:, c * width:(c + 1) * width]
        sq = a * a
        hi = sq.astype(BF16)
        lo = (sq - hi.astype(F32)).astype(BF16)
        ss = (jnp.dot(hi, ones_bd, preferred_element_type=F32)
              + jnp.dot(lo, ones_bd, preferred_element_type=F32))
        outs.append(a * lax.rsqrt(ss * (1.0 / ATT_HALF_DIM) + EPS))
    return jnp.concatenate(outs, axis=1) * rows[0]


def _ep_sigmoid_bias(accs, tiles, rows, consts):
    return 1.0 / (1.0 + jnp.exp(-(accs[0] + rows[0])))


def _ep_gated_merge(accs, tiles, rows, consts):
    return tiles[0].astype(F32) * accs[0] + tiles[1].astype(F32) * accs[1]


def _ep_residual(accs, tiles, rows, consts):
    return tiles[0] + accs[0]


def _ep_swiglu(accs, tiles, rows, consts):
    g = accs[0]
    return (g * (1.0 / (1.0 + jnp.exp(-g)))) * accs[1]


def _conv_kernel(cur_ref, prev_ref, next_ref, meta_ref, w_ref, b_ref,
                 dtr_ref, dtm_ref, dtb_ref,
                 xs_ref, bm_ref, cm_ref, dt_ref, ext_ref, *, n_chunks):
    c = pl.program_id(1)
    pad = SSD_CHUNK - N_META
    ch = ext_ref.shape[1]

    @pl.when(c == 0)
    def _():
        ext_ref[0:HALO + pad, :] = jnp.zeros((HALO + pad, ch), F32)
        ext_ref[HALO + pad:HALO + SSD_CHUNK, :] = meta_ref[...].astype(F32)

    @pl.when(c == 1)
    def _():
        ext_ref[0:HALO, :] = meta_ref[...].astype(F32)

    @pl.when(c > 1)
    def _():
        ext_ref[0:HALO, :] = prev_ref[0].astype(F32)

    @pl.when(c > 0)
    def _():
        ext_ref[HALO:HALO + SSD_CHUNK, :] = cur_ref[0].astype(F32)

    @pl.when(c < n_chunks - 1)
    def _():
        ext_ref[HALO + SSD_CHUNK:, :] = next_ref[0].astype(F32)

    @pl.when(c == n_chunks - 1)
    def _():
        ext_ref[HALO + SSD_CHUNK:, :] = jnp.zeros((HALO, ch), F32)

    half = (SSD_CONV - 1) // 2
    row = lax.broadcasted_iota(jnp.int32, (SSD_CHUNK, 1), 0)
    valid = jnp.logical_or(c > 0, row >= pad)

    def conv_silu(lo, hi):
        acc = None
        for k in range(SSD_CONV):
            x = ext_ref[HALO + k - half:HALO + k - half + SSD_CHUNK, lo:hi]
            term = x * w_ref[k:k + 1, lo:hi]
            acc = term if acc is None else acc + term
        acc = acc + b_ref[:, lo:hi]
        y = acc * (1.0 / (1.0 + jnp.exp(-acc)))
        return jnp.where(valid, y, 0.0)

    blk = 1024
    for s in range(D_MODEL // blk):
        xs_ref[0, :, s * blk:(s + 1) * blk] = conv_silu(s * blk, (s + 1) * blk).astype(xs_ref.dtype)
    bm_ref[0] = conv_silu(D_MODEL, D_MODEL + SSD_GN).astype(bm_ref.dtype)
    cm_ref[0] = conv_silu(D_MODEL + SSD_GN, SSD_CONV_CH).astype(cm_ref.dtype)

    raw_real = dtr_ref[0]
    raw_meta = jnp.concatenate(
        [jnp.zeros((pad, raw_real.shape[1]), F32), dtm_ref[...]], axis=0)
    raw = jnp.where(c > 0, raw_real, raw_meta) + dtb_ref[...]
    sp = jnp.maximum(raw, 0.0) + jnp.log1p(jnp.exp(-jnp.abs(raw)))
    dt_ref[0] = jnp.where(valid, sp, 0.0)


def _conv_dt(xbc, xbc_meta, conv_w, conv_b, dt_raw, dt_raw_meta, dt_bias):
    b, s, ch = xbc.shape
    n_chunks = s // SSD_CHUNK + 1
    lp = n_chunks * SSD_CHUNK
    hb = SSD_CHUNK // HALO
    n_halo = s // HALO
    nd = dt_raw.shape[-1]
    kern = functools.partial(_conv_kernel, n_chunks=n_chunks)
    return pl.pallas_call(
        kern,
        grid=(b, n_chunks),
        in_specs=[
            pl.BlockSpec((1, SSD_CHUNK, ch), lambda i, c: (i, jnp.maximum(c - 1, 0), 0)),
            pl.BlockSpec((1, HALO, ch), lambda i, c: (i, jnp.maximum((c - 1) * hb - 1, 0), 0)),
            pl.BlockSpec((1, HALO, ch), lambda i, c: (i, jnp.minimum(c * hb, n_halo - 1), 0)),
            pl.BlockSpec((N_META, ch), lambda i, c: (0, 0)),
            pl.BlockSpec((SSD_CONV, ch), lambda i, c: (0, 0)),
            pl.BlockSpec((1, ch), lambda i, c: (0, 0)),
            pl.BlockSpec((1, SSD_CHUNK, nd), lambda i, c: (i, jnp.maximum(c - 1, 0), 0)),
            pl.BlockSpec((N_META, nd), lambda i, c: (0, 0)),
            pl.BlockSpec((1, nd), lambda i, c: (0, 0)),
        ],
        out_specs=[
            pl.BlockSpec((1, SSD_CHUNK, D_MODEL), lambda i, c: (i, c, 0)),
            pl.BlockSpec((1, SSD_CHUNK, SSD_GN), lambda i, c: (i, c, 0)),
            pl.BlockSpec((1, SSD_CHUNK, SSD_GN), lambda i, c: (i, c, 0)),
            pl.BlockSpec((1, SSD_CHUNK, nd), lambda i, c: (i, c, 0)),
        ],
        out_shape=[
            jax.ShapeDtypeStruct((b, lp, D_MODEL), BF16),
            jax.ShapeDtypeStruct((b, lp, SSD_GN), BF16),
            jax.ShapeDtypeStruct((b, lp, SSD_GN), BF16),
            jax.ShapeDtypeStruct((b, lp, nd), F32),
        ],
        scratch_shapes=[pltpu.VMEM((2 * HALO + SSD_CHUNK, ch), F32)],
        compiler_params=_params(2, 40),
    )(xbc, xbc, xbc, xbc_meta, conv_w, conv_b.reshape(1, ch),
      dt_raw, dt_raw_meta, dt_bias.reshape(1, nd))


def _split3(x):
    hi = x.astype(BF16)
    r = x - hi.astype(F32)
    mid = r.astype(BF16)
    lo = (r - mid.astype(F32)).astype(BF16)
    return hi, mid, lo


def _ssd_body(x_ref, b_ref, c_ref, dt_ref, alog_ref, exp_ref, state_ref, *, reverse):
    t = SSD_CHUNK
    lane0 = SSD_N_HEADS if reverse else 0
    dt = dt_ref[0]
    a = dt * (-jnp.exp(alog_ref[...]))
    ri = lax.broadcasted_iota(jnp.int32, (t, t), 0)
    ci = lax.broadcasted_iota(jnp.int32, (t, t), 1)
    causal = (ci >= ri) if reverse else (ci <= ri)
    tri = causal.astype(BF16)
    a3 = _split3(a)
    cs = sum(jnp.dot(tri, p, preferred_element_type=F32) for p in a3)
    cs_t = cs.T
    last = t - 1
    edge = 0 if reverse else last
    cs_edge = cs[edge:edge + 1, :]
    decay_s = jnp.exp(cs_edge - cs)
    ecs = jnp.exp(cs)
    edge_parts = jnp.concatenate(
        [p.astype(F32) for p in _split3(ecs[edge:edge + 1, :])]
        + [jnp.zeros((HALO - 3, LANES), F32)], axis=0)
    stacked = jnp.concatenate([dt, decay_s, ecs, edge_parts], axis=0).astype(BF16)
    expanded = jnp.dot(stacked, exp_ref[...], preferred_element_type=F32)
    chunk_decay_all = (expanded[3 * t:3 * t + 1] + expanded[3 * t + 1:3 * t + 2]
                       + expanded[3 * t + 2:3 * t + 3])
    lane = lax.broadcasted_iota(jnp.int32, (t, LANES), 1)
    first_half = lane < SSD_HEAD_DIM
    outs = []
    for g in range(SSD_N_GROUPS):
        gs = slice(g * SSD_GROUP_W, (g + 1) * SSD_GROUP_W)
        ns = slice(g * SSD_D_STATE, (g + 1) * SSD_D_STATE)
        bg = b_ref[0, :, ns]
        cg = c_ref[0, :, ns]
        cb = lax.dot_general(cg, bg, (((1,), (1,)), ((), ())), preferred_element_type=F32)
        xdt = x_ref[0, :, gs].astype(F32) * expanded[0:t, gs]
        xdt_bf = xdt.astype(BF16)
        xdtd = (xdt * expanded[t:2 * t, gs]).astype(BF16)
        new_state = lax.dot_general(bg, xdtd, (((0,), (0,)), ((), ())),
                                    preferred_element_type=F32)
        prev = state_ref[g]
        y_off = jnp.dot(cg, prev.astype(BF16), preferred_element_type=F32) * expanded[2 * t:3 * t, gs]
        state_ref[g] = prev * chunk_decay_all[:, gs] + new_state
        for jp in range(SSD_HEADS_PER_GROUP // 2):
            ys = []
            xpair = xdt_bf[:, jp * LANES:(jp + 1) * LANES]
            for e in range(2):
                hl = lane0 + g * SSD_HEADS_PER_GROUP + 2 * jp + e
                col = cs[:, hl:hl + 1]
                rowv = cs_t[hl:hl + 1, :]
                seg = jnp.where(causal, col - rowv, 0.0)
                mh = jnp.where(causal, cb * jnp.exp(seg), 0.0).astype(BF16)
                ys.append(jnp.dot(mh, xpair, preferred_element_type=F32))
            y_pair = jnp.where(first_half, ys[0], ys[1]) + y_off[:, jp * LANES:(jp + 1) * LANES]
            outs.append(y_pair)
    return outs


def _ssd_rev_kernel(x_ref, b_ref, c_ref, dt_ref, alog_ref, exp_ref, y_ref, state_ref, *, n_chunks):
    step = pl.program_id(1)

    @pl.when(step == 0)
    def _():
        state_ref[...] = jnp.zeros(state_ref.shape, F32)

    outs = _ssd_body(x_ref, b_ref, c_ref, dt_ref, alog_ref, exp_ref, state_ref, reverse=True)

    @pl.when(step < n_chunks - 1)
    def _():
        for k, y in enumerate(outs):
            y_ref[0, :, k * LANES:(k + 1) * LANES] = y


def _ssd_fwd_kernel(x_ref, b_ref, c_ref, dt_ref, alog_ref, exp_ref, yb_ref, z_ref,
                    dskip_ref, gain_ref, o_ref, state_ref, ybuf_ref):
    step = pl.program_id(1)

    @pl.when(step == 0)
    def _():
        state_ref[...] = jnp.zeros(state_ref.shape, F32)

    outs = _ssd_body(x_ref, b_ref, c_ref, dt_ref, alog_ref, exp_ref, state_ref, reverse=False)

    @pl.when(step > 0)
    def _():
        for k, y in enumerate(outs):
            ls = slice(k * LANES, (k + 1) * LANES)
            x = x_ref[0, :, ls].astype(F32)
            z = z_ref[0, :, ls].astype(F32)
            yk = (y + yb_ref[0, :, ls]) + x * dskip_ref[:, ls]
            ybuf_ref[:, ls] = yk * (z * (1.0 / (1.0 + jnp.exp(-z))))
        for g in range(SSD_N_GROUPS):
            gs = slice(g * SSD_GROUP_W, (g + 1) * SSD_GROUP_W)
            yg = ybuf_ref[:, gs]
            ms = jnp.mean(yg * yg, axis=-1, keepdims=True)
            o_ref[0, :, gs] = (yg * lax.rsqrt(ms + EPS) * gain_ref[:, gs]).astype(o_ref.dtype)


def _ssd(xs, bmat, cmat, dt, a_log, d_skip, ssd_norm_g, z):
    b, lp, _ = xs.shape
    n_chunks = lp // SSD_CHUNK
    s = lp - SSD_CHUNK
    nd = dt.shape[-1]
    alog = a_log.reshape(1, nd)
    heads = jnp.arange(D_MODEL, dtype=jnp.int32) // SSD_HEAD_DIM
    lanes = jnp.arange(nd, dtype=jnp.int32)
    expand_f = (lanes[:, None] == heads[None, :]).astype(BF16)
    expand_r = (lanes[:, None] == heads[None, :] + SSD_N_HEADS).astype(BF16)
    dskip = jnp.repeat(d_skip, SSD_HEAD_DIM).reshape(1, D_MODEL)
    gain = ssd_norm_g.reshape(1, D_MODEL)
    state = pltpu.VMEM((SSD_N_GROUPS, SSD_D_STATE, SSD_GROUP_W), F32)

    def chunk_specs(cmap):
        return [
            pl.BlockSpec((1, SSD_CHUNK, D_MODEL), lambda i, c: (i, cmap(c), 0)),
            pl.BlockSpec((1, SSD_CHUNK, SSD_GN), lambda i, c: (i, cmap(c), 0)),
            pl.BlockSpec((1, SSD_CHUNK, SSD_GN), lambda i, c: (i, cmap(c), 0)),
            pl.BlockSpec((1, SSD_CHUNK, nd), lambda i, c: (i, cmap(c), 0)),
            pl.BlockSpec((1, nd), lambda i, c: (0, 0)),
            pl.BlockSpec((nd, D_MODEL), lambda i, c: (0, 0)),
        ]

    rmap = lambda c: n_chunks - 1 - c
    rout = lambda c: jnp.maximum(n_chunks - 2 - c, 0)
    y_rev = pl.pallas_call(
        functools.partial(_ssd_rev_kernel, n_chunks=n_chunks),
        grid=(b, n_chunks),
        in_specs=chunk_specs(rmap),
        out_specs=pl.BlockSpec((1, SSD_CHUNK, D_MODEL), lambda i, c: (i, rout(c), 0)),
        out_shape=jax.ShapeDtypeStruct((b, s, D_MODEL), F32),
        scratch_shapes=[state],
        compiler_params=_params(2, 48),
    )(xs, bmat, cmat, dt, alog, expand_r)

    fmap = lambda c: c
    fout = lambda c: jnp.maximum(c - 1, 0)
    return pl.pallas_call(
        _ssd_fwd_kernel,
        grid=(b, n_chunks),
        in_specs=chunk_specs(fmap) + [
            pl.BlockSpec((1, SSD_CHUNK, D_MODEL), lambda i, c: (i, fout(c), 0)),
            pl.BlockSpec((1, SSD_CHUNK, D_MODEL), lambda i, c: (i, fout(c), 0)),
            pl.BlockSpec((1, D_MODEL), lambda i, c: (0, 0)),
            pl.BlockSpec((1, D_MODEL), lambda i, c: (0, 0)),
        ],
        out_specs=pl.BlockSpec((1, SSD_CHUNK, D_MODEL), lambda i, c: (i, fout(c), 0)),
        out_shape=jax.ShapeDtypeStruct((b, s, D_MODEL), BF16),
        scratch_shapes=[state, pltpu.VMEM((SSD_CHUNK, D_MODEL), F32)],
        compiler_params=_params(2, 48),
    )(xs, bmat, cmat, dt, alog, expand_f, y_rev, z, dskip, gain)


def _attn_kernel(slopes_ref, q_ref, k_ref, v_ref, kt_ref, vt_ref,
                 lq1_ref, lk1_ref, lq2_ref, lk2_ref, sub_ref, o_ref,
                 *, tq, tk, n_kt, lam_init):
    h = pl.program_id(1)
    i = pl.program_id(2)
    slope = slopes_ref[h]
    q = q_ref[0]
    lane = lax.broadcasted_iota(jnp.int32, q.shape, 1)
    zero = jnp.zeros_like(q)
    qs = (jnp.where(lane < ATT_HALF_DIM, q, zero), jnp.where(lane >= ATT_HALF_DIM, q, zero))
    q_pos = (N_META + i * tq + lax.broadcasted_iota(jnp.int32, (tq, 1), 0)).astype(F32)

    def scores(qh, k):
        return lax.dot_general(qh, k, (((1,), (1,)), ((), ())), preferred_element_type=F32)

    def update(carry, s, v):
        m, l, acc = carry
        m_new = jnp.maximum(m, jnp.max(s, axis=-1, keepdims=True))
        alpha = jnp.exp(m - m_new)
        p = jnp.exp(s - m_new)
        l_new = alpha * l + jnp.sum(p, axis=-1, keepdims=True)
        acc_new = alpha * acc + jnp.dot(p.astype(BF16), v, preferred_element_type=F32)
        return m_new, l_new, acc_new

    kt = kt_ref[...]
    vt = vt_ref[...]
    tcol = lax.broadcasted_iota(jnp.int32, (1, kt.shape[0]), 1)
    t_bias = slope * jnp.abs(q_pos - tcol.astype(F32))
    t_valid = tcol < N_META
    carries = []
    for e in range(2):
        s = jnp.where(t_valid, scores(qs[e], kt) - t_bias, -jnp.inf)
        m0 = jnp.max(s, axis=-1, keepdims=True)
        p = jnp.exp(s - m0)
        l0 = jnp.sum(p, axis=-1, keepdims=True)
        acc0 = jnp.dot(p.astype(BF16), vt, preferred_element_type=F32)
        carries.append((m0, l0, acc0))

    kcol = lax.broadcasted_iota(jnp.int32, (1, tk), 1)

    def body(t, carries):
        start = pl.multiple_of(t * tk, tk)
        k = k_ref[0, pl.ds(start, tk), :]
        v = v_ref[0, pl.ds(start, tk), :]
        k_pos = (N_META + start + kcol).astype(F32)
        bias = slope * jnp.abs(q_pos - k_pos)
        return tuple(update(carries[e], scores(qs[e], k) - bias, v) for e in range(2))

    carries = lax.fori_loop(0, n_kt, body, tuple(carries))

    lam = (jnp.exp(jnp.sum(lq1_ref[...] * lk1_ref[...], axis=-1, keepdims=True))
           - jnp.exp(jnp.sum(lq2_ref[...] * lk2_ref[...], axis=-1, keepdims=True))
           + lam_init)
    (m1, l1, a1), (m2, l2, a2) = carries
    o = a1 / l1 - lam * (a2 / l2)
    ms = jnp.mean(o * o, axis=-1, keepdims=True)
    o = (o * lax.rsqrt(ms + EPS) * sub_ref[...]) * (1.0 - lam_init)
    o_ref[0] = o.astype(o_ref.dtype)


def _attention(q, k, v, k_tail, v_tail, lambdas, subln_g, layer_idx):
    b, s, _ = q.shape
    tq = min(512, s)
    tk = min(512, s)
    lam_init = 0.8 - 0.6 * math.exp(-0.3 * layer_idx)
    slopes = jnp.exp2(-8.0 * jnp.arange(1, ATT_N_HEADS + 1, dtype=F32) / ATT_N_HEADS)
    kern = functools.partial(_attn_kernel, tq=tq, tk=tk, n_kt=s // tk, lam_init=lam_init)
    vec = lambda x: x.reshape(1, -1)
    small = lambda n: pl.BlockSpec((1, n), lambda bi, h, i: (0, 0))
    return pl.pallas_call(
        kern,
        grid=(b, ATT_N_HEADS, s // tq),
        in_specs=[
            pl.BlockSpec(memory_space=pltpu.SMEM),
            pl.BlockSpec((1, tq, LANES), lambda bi, h, i: (bi, i, h)),
            pl.BlockSpec((1, s, LANES), lambda bi, h, i: (bi, 0, h)),
            pl.BlockSpec((1, s, LANES), lambda bi, h, i: (bi, 0, h)),
            pl.BlockSpec((LANES, LANES), lambda bi, h, i: (0, h)),
            pl.BlockSpec((LANES, LANES), lambda bi, h, i: (0, h)),
            small(ATT_HALF_DIM), small(ATT_HALF_DIM), small(ATT_HALF_DIM), small(ATT_HALF_DIM),
            small(ATT_V_DIM),
        ],
        out_specs=pl.BlockSpec((1, tq, LANES), lambda bi, h, i: (bi, i, h)),
        out_shape=jax.ShapeDtypeStruct((b, s, ATT_WIDTH), BF16),
        compiler_params=_params(3, 40),
    )(slopes, q, k, v, k_tail, v_tail, *[vec(x) for x in lambdas], vec(subln_g))


def _proj(h_in, w, n, dtype, ep=None, **kw):
    return _matmul([h_in], [(w, 0)], [(0, 0)], ep or _ep_plain, n, dtype, 1024, min(1024, n), 52, **kw)


def _prepare(meta_tokens, wts):
    bf = lambda w: w.astype(BF16)
    o1 = D_MODEL
    o2 = o1 + SSD_CONV_CH
    o3 = o2 + 2 * SSD_N_HEADS
    o4 = o3 + ATT_WIDTH
    o5 = o4 + ATT_WIDTH
    w_in = wts["w_in"]
    p = dict(wts)
    p["w_z"], p["w_xbc"], p["w_dt"] = bf(w_in[:, :o1]), bf(w_in[:, o1:o2]), bf(w_in[:, o2:o3])
    p["w_q"], p["w_k"], p["w_v"] = bf(w_in[:, o3:o4]), bf(w_in[:, o4:o5]), bf(w_in[:, o5:])
    for name in ("w_branch_gate", "w_ssd_out", "w_att_out", "w_o", "w_gate_up", "w_down"):
        p[name] = bf(wts[name])
    ones_bd = (jnp.arange(256)[:, None] // ATT_HALF_DIM
               == jnp.arange(256)[None, :] // ATT_HALF_DIM).astype(BF16)
    p["qk_norm"] = dict(
        q=dict(rows=[(jnp.tile(wts["q_norm_g"] * (ATT_HALF_DIM ** -0.5),
                               2 * ATT_N_HEADS).reshape(1, ATT_WIDTH), 0)], consts=[ones_bd]),
        k=dict(rows=[(jnp.tile(wts["k_norm_g"], 2 * ATT_N_HEADS).reshape(1, ATT_WIDTH), 0)],
               consts=[ones_bd]))
    hn_meta = _rmsnorm(meta_tokens, wts["norm_mix_g"], N_META)
    tail_pad = ((0, LANES - N_META), (0, 0))
    p["xbc_meta"] = _proj(hn_meta, p["w_xbc"], SSD_CONV_CH, BF16)
    p["dt_raw_meta"] = _proj(hn_meta, p["w_dt"], 2 * SSD_N_HEADS, F32)
    p["k_tail"] = jnp.pad(_proj(hn_meta, p["w_k"], ATT_WIDTH, BF16, ep=_ep_qknorm,
                                **p["qk_norm"]["k"]), tail_pad)
    p["v_tail"] = jnp.pad(_proj(hn_meta, p["w_v"], ATT_WIDTH, BF16), tail_pad)
    return p


def _layer(x, layer_idx, p):
    b, s, d = x.shape
    m = b * s
    x2 = x.reshape(m, d)
    hn = _rmsnorm(x2, p["norm_mix_g"], 256)

    z = _proj(hn, p["w_z"], D_MODEL, BF16)
    xbc = _proj(hn, p["w_xbc"], SSD_CONV_CH, BF16)
    dt_raw = _proj(hn, p["w_dt"], 2 * SSD_N_HEADS, F32)
    q = _proj(hn, p["w_q"], ATT_WIDTH, BF16, ep=_ep_qknorm, **p["qk_norm"]["q"])
    k = _proj(hn, p["w_k"], ATT_WIDTH, BF16, ep=_ep_qknorm, **p["qk_norm"]["k"])
    v = _proj(hn, p["w_v"], ATT_WIDTH, BF16)

    xs, bmat, cmat, dt = _conv_dt(
        xbc.reshape(b, s, SSD_CONV_CH), p["xbc_meta"], p["conv_w"], p["conv_b"],
        dt_raw.reshape(b, s, 2 * SSD_N_HEADS), p["dt_raw_meta"], p["dt_bias"].reshape(-1))
    y_ssd = _ssd(xs, bmat, cmat, dt, p["a_log"].reshape(-1), p["d_skip"],
                 p["ssd_norm_g"], z.reshape(b, s, D_MODEL))

    y_att = _attention(
        q.reshape(b, s, ATT_WIDTH), k.reshape(b, s, ATT_WIDTH), v.reshape(b, s, ATT_WIDTH),
        p["k_tail"], p["v_tail"],
        (p["lambda_q1"], p["lambda_k1"], p["lambda_q2"], p["lambda_k2"]),
        p["subln_g"], layer_idx)

    gates = _matmul([hn], [(p["w_branch_gate"], 0)], [(0, 0)], _ep_sigmoid_bias,
                    2 * D_MODEL, BF16, 1024, 1024, 52,
                    rows=[(p["b_branch_gate"].reshape(1, -1), 0)])
    bn_mix = 512
    mixed = _matmul([y_ssd.reshape(m, D_MODEL), y_att.reshape(m, ATT_WIDTH)],
                    [(p["w_ssd_out"], 0), (p["w_att_out"], 0)],
                    [(0, 0), (1, 1)], _ep_gated_merge, D_MODEL, BF16, 1024, bn_mix, 56,
                    tiles=[(gates, 0), (gates, D_MODEL // bn_mix)])
    h2 = _matmul([mixed], [(p["w_o"], 0)], [(0, 0)], _ep_residual,
                 D_MODEL, F32, 1024, 512, 48, tiles=[(x2, 0)])

    hf = _rmsnorm(h2, p["norm_ffn_g"], 256)
    bn_ff = 256
    act = _matmul([hf], [(p["w_gate_up"], 0), (p["w_gate_up"], D_FF // bn_ff)],
                  [(0, 0), (0, 1)], _ep_swiglu, D_FF, BF16, 1024, bn_ff, 40)
    out = _matmul([act], [(p["w_down"], 0)], [(0, 0)], _ep_residual,
                  D_MODEL, F32, 512, 256, 48, tiles=[(h2, 0)])
    return out.reshape(b, s, d)


def kernel(x_prompt, x_sample, meta_tokens, norm_mix_g, w_in, conv_w, conv_b, dt_bias, a_log, d_skip, ssd_norm_g, q_norm_g, k_norm_g, lambda_q1, lambda_k1, lambda_q2, lambda_k2, subln_g, w_branch_gate, b_branch_gate, w_ssd_out, w_att_out, w_o, norm_ffn_g, w_gate_up, w_down):
    names = ("norm_mix_g", "w_in", "conv_w", "conv_b", "dt_bias", "a_log", "d_skip",
             "ssd_norm_g", "q_norm_g", "k_norm_g", "lambda_q1", "lambda_k1", "lambda_q2",
             "lambda_k2", "subln_g", "w_branch_gate", "b_branch_gate", "w_ssd_out",
             "w_att_out", "w_o", "norm_ffn_g", "w_gate_up", "w_down")
    stacked = (norm_mix_g, w_in, conv_w, conv_b, dt_bias, a_log, d_skip, ssd_norm_g,
               q_norm_g, k_norm_g, lambda_q1, lambda_k1, lambda_q2, lambda_k2, subln_g,
               w_branch_gate, b_branch_gate, w_ssd_out, w_att_out, w_o, norm_ffn_g,
               w_gate_up, w_down)
    assert all(w.shape[0] == 1 for w in stacked), "single-layer encoder"
    p = _prepare(meta_tokens, {n: w[0] for n, w in zip(names, stacked)})
    return (_layer(x_prompt, 0, p), _layer(x_sample, 0, p))
```

```python
import functools
import math

import jax
import jax.numpy as jnp
from jax import lax
from jax.experimental import pallas as pl
from jax.experimental.pallas import tpu as pltpu

F32 = jnp.float32
BF16 = jnp.bfloat16

D_MODEL = 4096
N_META = 16
EPS = 1e-6
SSD_HEAD_DIM = 64
SSD_N_HEADS = D_MODEL // SSD_HEAD_DIM
SSD_N_GROUPS = 8
SSD_HEADS_PER_GROUP = SSD_N_HEADS // SSD_N_GROUPS
SSD_D_STATE = 128
SSD_CONV = 5
SSD_CHUNK = 128
SSD_GN = SSD_N_GROUPS * SSD_D_STATE
SSD_CONV_CH = D_MODEL + 2 * SSD_GN
SSD_GROUP_W = SSD_HEADS_PER_GROUP * SSD_HEAD_DIM
ATT_N_HEADS = 16
ATT_HALF_DIM = 64
ATT_V_DIM = 128
ATT_WIDTH = ATT_N_HEADS * ATT_V_DIM
D_FF = 11008
LANES = 128
HALO = 16
MIB = 1024 * 1024


def _params(n_axes, vmem_mib):
    return pltpu.CompilerParams(
        dimension_semantics=("arbitrary",) * n_axes,
        vmem_limit_bytes=vmem_mib * MIB)


def _rmsnorm_kernel(x_ref, g_ref, o_ref):
    x = x_ref[...]
    ms = jnp.mean(x * x, axis=-1, keepdims=True)
    o_ref[...] = (x * lax.rsqrt(ms + EPS) * g_ref[...]).astype(o_ref.dtype)


def _rmsnorm(x, g, bm):
    m, d = x.shape
    bm = min(bm, m)
    return pl.pallas_call(
        _rmsnorm_kernel,
        grid=(m // bm,),
        in_specs=[pl.BlockSpec((bm, d), lambda i: (i, 0)),
                  pl.BlockSpec((1, d), lambda i: (0, 0))],
        out_specs=pl.BlockSpec((bm, d), lambda i: (i, 0)),
        out_shape=jax.ShapeDtypeStruct((m, d), BF16),
        compiler_params=_params(1, 32),
    )(x, g.reshape(1, d))


def _mm_kernel(*refs, dots, n_lhs, n_rhs, n_tiles, n_rows, n_consts, epilogue):
    lhs = refs[:n_lhs]
    rhs = refs[n_lhs:n_lhs + n_rhs]
    p = n_lhs + n_rhs
    tiles = refs[p:p + n_tiles]
    p += n_tiles
    rows = refs[p:p + n_rows]
    p += n_rows
    consts = refs[p:p + n_consts]
    o_ref = refs[-1]
    accs = [jnp.dot(lhs[a][...], rhs[b][...], preferred_element_type=F32)
            for a, b in dots]
    out = epilogue(accs, [t[...] for t in tiles], [r[...] for r in rows],
                   [c[...] for c in consts])
    o_ref[...] = out.astype(o_ref.dtype)


def _matmul(lhs, rhs, dots, epilogue, n_out, out_dtype, bm, bn, vmem_mib,
            tiles=(), rows=(), consts=()):
    m = lhs[0].shape[0]
    bm = min(bm, m)
    in_specs = []
    args = []
    for a in lhs:
        in_specs.append(pl.BlockSpec((bm, a.shape[1]), lambda i, j: (i, 0)))
        args.append(a)
    for w, off in rhs:
        in_specs.append(pl.BlockSpec((w.shape[0], bn), lambda i, j, off=off: (0, j + off)))
        args.append(w)
    for t, off in tiles:
        in_specs.append(pl.BlockSpec((bm, bn), lambda i, j, off=off: (i, j + off)))
        args.append(t)
    for r, off in rows:
        in_specs.append(pl.BlockSpec((1, bn), lambda i, j, off=off: (0, j + off)))
        args.append(r)
    for c in consts:
        in_specs.append(pl.BlockSpec(c.shape, lambda i, j, nd=c.ndim: (0,) * nd))
        args.append(c)
    kern = functools.partial(
        _mm_kernel, dots=tuple(dots), n_lhs=len(lhs), n_rhs=len(rhs),
        n_tiles=len(tiles), n_rows=len(rows), n_consts=len(consts), epilogue=epilogue)
    return pl.pallas_call(
        kern,
        grid=(m // bm, n_out // bn),
        in_specs=in_specs,
        out_specs=pl.BlockSpec((bm, bn), lambda i, j: (i, j)),
        out_shape=jax.ShapeDtypeStruct((m, n_out), out_dtype),
        compiler_params=_params(2, vmem_mib),
    )(*args)


def _ep_plain(accs, tiles, rows, consts):
    return accs[0]


def _ep_qknorm(accs, tiles, rows, consts):
    acc = accs[0]
    ones_bd = consts[0]
    width = ones_bd.shape[0]
    outs = []
    for c in range(acc.shape[1] // width):
        a = acc[:, c * width:(c + 1) * width]
        sq = a * a
        hi = sq.astype(BF16)
        lo = (sq - hi.astype(F32)).astype(BF16)
        ss = (jnp.dot(hi, ones_bd, preferred_element_type=F32)
              + jnp.dot(lo, ones_bd, preferred_element_type=F32))
        outs.append(a * lax.rsqrt(ss * (1.0 / ATT_HALF_DIM) + EPS))
    return jnp.concatenate(outs, axis=1) * rows[0]


def _ep_sigmoid_bias(accs, tiles, rows, consts):
    return 1.0 / (1.0 + jnp.exp(-(accs[0] + rows[0])))


def _ep_gated_merge(accs, tiles, rows, consts):
    return tiles[0].astype(F32) * accs[0] + tiles[1].astype(F32) * accs[1]


def _ep_residual(accs, tiles, rows, consts):
    return tiles[0] + accs[0]


def _ep_swiglu(accs, tiles, rows, consts):
    g = accs[0]
    return (g * (1.0 / (1.0 + jnp.exp(-g)))) * accs[1]


def _conv_kernel(cur_ref, prev_ref, next_ref, meta_ref, w_ref, b_ref,
                 dtr_ref, dtm_ref, dtb_ref,
                 xs_ref, bm_ref, cm_ref, dt_ref, ext_ref, *, n_chunks):
    c = pl.program_id(1)
    pad = SSD_CHUNK - N_META
    ch = ext_ref.shape[1]

    @pl.when(c == 0)
    def _():
        ext_ref[0:HALO + pad, :] = jnp.zeros((HALO + pad, ch), F32)
        ext_ref[HALO + pad:HALO + SSD_CHUNK, :] = meta_ref[...].astype(F32)

    @pl.when(c == 1)
    def _():
        ext_ref[0:HALO, :] = meta_ref[...].astype(F32)

    @pl.when(c > 1)
    def _():
        ext_ref[0:HALO, :] = prev_ref[0].astype(F32)

    @pl.when(c > 0)
    def _():
        ext_ref[HALO:HALO + SSD_CHUNK, :] = cur_ref[0].astype(F32)

    @pl.when(c < n_chunks - 1)
    def _():
        ext_ref[HALO + SSD_CHUNK:, :] = next_ref[0].astype(F32)

    @pl.when(c == n_chunks - 1)
    def _():
        ext_ref[HALO + SSD_CHUNK:, :] = jnp.zeros((HALO, ch), F32)

    half = (SSD_CONV - 1) // 2
    row = lax.broadcasted_iota(jnp.int32, (SSD_CHUNK, 1), 0)
    valid = jnp.logical_or(c > 0, row >= pad)

    def conv_silu(lo, hi):
        acc = None
        for k in range(SSD_CONV):
            x = ext_ref[HALO + k - half:HALO + k - half + SSD_CHUNK, lo:hi]
            term = x * w_ref[k:k + 1, lo:hi]
            acc = term if acc is None else acc + term
        acc = acc + b_ref[:, lo:hi]
        y = acc * (1.0 / (1.0 + jnp.exp(-acc)))
        return jnp.where(valid, y, 0.0)

    blk = 1024
    for s in range(D_MODEL // blk):
        xs_ref[0, :, s * blk:(s + 1) * blk] = conv_silu(s * blk, (s + 1) * blk).astype(xs_ref.dtype)
    bm_ref[0] = conv_silu(D_MODEL, D_MODEL + SSD_GN).astype(bm_ref.dtype)
    cm_ref[0] = conv_silu(D_MODEL + SSD_GN, SSD_CONV_CH).astype(cm_ref.dtype)

    raw_real = dtr_ref[0]
    raw_meta = jnp.concatenate(
        [jnp.zeros((pad, raw_real.shape[1]), F32), dtm_ref[...]], axis=0)
    raw = jnp.where(c > 0, raw_real, raw_meta) + dtb_ref[...]
    sp = jnp.maximum(raw, 0.0) + jnp.log1p(jnp.exp(-jnp.abs(raw)))
    dt_ref[0] = jnp.where(valid, sp, 0.0)


def _conv_dt(xbc, xbc_meta, conv_w, conv_b, dt_raw, dt_raw_meta, dt_bias):
    b, s, ch = xbc.shape
    n_chunks = s // SSD_CHUNK + 1
    lp = n_chunks * SSD_CHUNK
    hb = SSD_CHUNK // HALO
    n_halo = s // HALO
    nd = dt_raw.shape[-1]
    kern = functools.partial(_conv_kernel, n_chunks=n_chunks)
    return pl.pallas_call(
        kern,
        grid=(b, n_chunks),
        in_specs=[
            pl.BlockSpec((1, SSD_CHUNK, ch), lambda i, c: (i, jnp.maximum(c - 1, 0), 0)),
            pl.BlockSpec((1, HALO, ch), lambda i, c: (i, jnp.maximum((c - 1) * hb - 1, 0), 0)),
            pl.BlockSpec((1, HALO, ch), lambda i, c: (i, jnp.minimum(c * hb, n_halo - 1), 0)),
            pl.BlockSpec((N_META, ch), lambda i, c: (0, 0)),
            pl.BlockSpec((SSD_CONV, ch), lambda i, c: (0, 0)),
            pl.BlockSpec((1, ch), lambda i, c: (0, 0)),
            pl.BlockSpec((1, SSD_CHUNK, nd), lambda i, c: (i, jnp.maximum(c - 1, 0), 0)),
            pl.BlockSpec((N_META, nd), lambda i, c: (0, 0)),
            pl.BlockSpec((1, nd), lambda i, c: (0, 0)),
        ],
        out_specs=[
            pl.BlockSpec((1, SSD_CHUNK, D_MODEL), lambda i, c: (i, c, 0)),
            pl.BlockSpec((1, SSD_CHUNK, SSD_GN), lambda i, c: (i, c, 0)),
            pl.BlockSpec((1, SSD_CHUNK, SSD_GN), lambda i, c: (i, c, 0)),
            pl.BlockSpec((1, SSD_CHUNK, nd), lambda i, c: (i, c, 0)),
        ],
        out_shape=[
            jax.ShapeDtypeStruct((b, lp, D_MODEL), BF16),
            jax.ShapeDtypeStruct((b, lp, SSD_GN), BF16),
            jax.ShapeDtypeStruct((b, lp, SSD_GN), BF16),
            jax.ShapeDtypeStruct((b, lp, nd), F32),
        ],
        scratch_shapes=[pltpu.VMEM((2 * HALO + SSD_CHUNK, ch), F32)],
        compiler_params=_params(2, 40),
    )(xbc, xbc, xbc, xbc_meta, conv_w, conv_b.reshape(1, ch),
      dt_raw, dt_raw_meta, dt_bias.reshape(1, nd))


def _split3(x):
    hi = x.astype(BF16)
    r = x - hi.astype(F32)
    mid = r.astype(BF16)
    lo = (r - mid.astype(F32)).astype(BF16)
    return hi, mid, lo


def _ssd_body(x_ref, b_ref, c_ref, dt_ref, alog_ref, exp_ref, state_ref, *, reverse):
    t = SSD_CHUNK
    lane0 = SSD_N_HEADS if reverse else 0
    dt = dt_ref[0]
    a = dt * (-jnp.exp(alog_ref[...]))
    ri = lax.broadcasted_iota(jnp.int32, (t, t), 0)
    ci = lax.broadcasted_iota(jnp.int32, (t, t), 1)
    causal = (ci >= ri) if reverse else (ci <= ri)
    tri = causal.astype(BF16)
    a3 = _split3(a)
    cs = sum(jnp.dot(tri, p, preferred_element_type=F32) for p in a3)
    cs_t = cs.T
    last = t - 1
    edge = 0 if reverse else last
    cs_edge = cs[edge:edge + 1, :]
    decay_s = jnp.exp(cs_edge - cs)
    ecs = jnp.exp(cs)
    edge_parts = jnp.concatenate(
        [p.astype(F32) for p in _split3(ecs[edge:edge + 1, :])]
        + [jnp.zeros((HALO - 3, LANES), F32)], axis=0)
    stacked = jnp.concatenate([dt, decay_s, ecs, edge_parts], axis=0).astype(BF16)
    expanded = jnp.dot(stacked, exp_ref[...], preferred_element_type=F32)
    chunk_decay_all = (expanded[3 * t:3 * t + 1] + expanded[3 * t + 1:3 * t + 2]
                       + expanded[3 * t + 2:3 * t + 3])
    lane = lax.broadcasted_iota(jnp.int32, (t, LANES), 1)
    first_half = lane < SSD_HEAD_DIM
    outs = []
    for g in range(SSD_N_GROUPS):
        gs = slice(g * SSD_GROUP_W, (g + 1) * SSD_GROUP_W)
        ns = slice(g * SSD_D_STATE, (g + 1) * SSD_D_STATE)
        bg = b_ref[0, :, ns]
        cg = c_ref[0, :, ns]
        cb = lax.dot_general(cg, bg, (((1,), (1,)), ((), ())), preferred_element_type=F32)
        xdt = x_ref[0, :, gs].astype(F32) * expanded[0:t, gs]
        xdt_bf = xdt.astype(BF16)
        xdtd = (xdt * expanded[t:2 * t, gs]).astype(BF16)
        new_state = lax.dot_general(bg, xdtd, (((0,), (0,)), ((), ())),
                                    preferred_element_type=F32)
        prev = state_ref[g]
        y_off = jnp.dot(cg, prev.astype(BF16), preferred_element_type=F32) * expanded[2 * t:3 * t, gs]
        state_ref[g] = prev * chunk_decay_all[:, gs] + new_state
        for jp in range(SSD_HEADS_PER_GROUP // 2):
            ys = []
            xpair = xdt_bf[:, jp * LANES:(jp + 1) * LANES]
            for e in range(2):
                hl = lane0 + g * SSD_HEADS_PER_GROUP + 2 * jp + e
                col = cs[:, hl:hl + 1]
                rowv = cs_t[hl:hl + 1, :]
                seg = jnp.where(causal, col - rowv, 0.0)
                mh = jnp.where(causal, cb * jnp.exp(seg), 0.0).astype(BF16)
                ys.append(jnp.dot(mh, xpair, preferred_element_type=F32))
            y_pair = jnp.where(first_half, ys[0], ys[1]) + y_off[:, jp * LANES:(jp + 1) * LANES]
            outs.append(y_pair)
    return outs


def _ssd_rev_kernel(x_ref, b_ref, c_ref, dt_ref, alog_ref, exp_ref, y_ref, state_ref, *, n_chunks):
    step = pl.program_id(1)

    @pl.when(step == 0)
    def _():
        state_ref[...] = jnp.zeros(state_ref.shape, F32)

    outs = _ssd_body(x_ref, b_ref, c_ref, dt_ref, alog_ref, exp_ref, state_ref, reverse=True)

    @pl.when(step < n_chunks - 1)
    def _():
        for k, y in enumerate(outs):
            y_ref[0, :, k * LANES:(k + 1) * LANES] = y


def _ssd_fwd_kernel(x_ref, b_ref, c_ref, dt_ref, alog_ref, exp_ref, yb_ref, z_ref,
                    dskip_ref, gain_ref, o_ref, state_ref, ybuf_ref):
    step = pl.program_id(1)

    @pl.when(step == 0)
    def _():
        state_ref[...] = jnp.zeros(state_ref.shape, F32)

    outs = _ssd_body(x_ref, b_ref, c_ref, dt_ref, alog_ref, exp_ref, state_ref, reverse=False)

    @pl.when(step > 0)
    def _():
        for k, y in enumerate(outs):
            ls = slice(k * LANES, (k + 1) * LANES)
            x = x_ref[0, :, ls].astype(F32)
            z = z_ref[0, :, ls].astype(F32)
            yk = (y + yb_ref[0, :, ls]) + x * dskip_ref[:, ls]
            ybuf_ref[:, ls] = yk * (z * (1.0 / (1.0 + jnp.exp(-z))))
        for g in range(SSD_N_GROUPS):
            gs = slice(g * SSD_GROUP_W, (g + 1) * SSD_GROUP_W)
            yg = ybuf_ref[:, gs]
            ms = jnp.mean(yg * yg, axis=-1, keepdims=True)
            o_ref[0, :, gs] = (yg * lax.rsqrt(ms + EPS) * gain_ref[:, gs]).astype(o_ref.dtype)


def _ssd(xs, bmat, cmat, dt, a_log, d_skip, ssd_norm_g, z):
    b, lp, _ = xs.shape
    n_chunks = lp // SSD_CHUNK
    s = lp - SSD_CHUNK
    nd = dt.shape[-1]
    alog = a_log.reshape(1, nd)
    heads = jnp.arange(D_MODEL, dtype=jnp.int32) // SSD_HEAD_DIM
    lanes = jnp.arange(nd, dtype=jnp.int32)
    expand_f = (lanes[:, None] == heads[None, :]).astype(BF16)
    expand_r = (lanes[:, None] == heads[None, :] + SSD_N_HEADS).astype(BF16)
    dskip = jnp.repeat(d_skip, SSD_HEAD_DIM).reshape(1, D_MODEL)
    gain = ssd_norm_g.reshape(1, D_MODEL)
    state = pltpu.VMEM((SSD_N_GROUPS, SSD_D_STATE, SSD_GROUP_W), F32)

    def chunk_specs(cmap):
        return [
            pl.BlockSpec((1, SSD_CHUNK, D_MODEL), lambda i, c: (i, cmap(c), 0)),
            pl.BlockSpec((1, SSD_CHUNK, SSD_GN), lambda i, c: (i, cmap(c), 0)),
            pl.BlockSpec((1, SSD_CHUNK, SSD_GN), lambda i, c: (i, cmap(c), 0)),
            pl.BlockSpec((1, SSD_CHUNK, nd), lambda i, c: (i, cmap(c), 0)),
            pl.BlockSpec((1, nd), lambda i, c: (0, 0)),
            pl.BlockSpec((nd, D_MODEL), lambda i, c: (0, 0)),
        ]

    rmap = lambda c: n_chunks - 1 - c
    rout = lambda c: jnp.maximum(n_chunks - 2 - c, 0)
    y_rev = pl.pallas_call(
        functools.partial(_ssd_rev_kernel, n_chunks=n_chunks),
        grid=(b, n_chunks),
        in_specs=chunk_specs(rmap),
        out_specs=pl.BlockSpec((1, SSD_CHUNK, D_MODEL), lambda i, c: (i, rout(c), 0)),
        out_shape=jax.ShapeDtypeStruct((b, s, D_MODEL), F32),
        scratch_shapes=[state],
        compiler_params=_params(2, 48),
    )(xs, bmat, cmat, dt, alog, expand_r)

    fmap = lambda c: c
    fout = lambda c: jnp.maximum(c - 1, 0)
    return pl.pallas_call(
        _ssd_fwd_kernel,
        grid=(b, n_chunks),
        in_specs=chunk_specs(fmap) + [
            pl.BlockSpec((1, SSD_CHUNK, D_MODEL), lambda i, c: (i, fout(c), 0)),
            pl.BlockSpec((1, SSD_CHUNK, D_MODEL), lambda i, c: (i, fout(c), 0)),
            pl.BlockSpec((1, D_MODEL), lambda i, c: (0, 0)),
            pl.BlockSpec((1, D_MODEL), lambda i, c: (0, 0)),
        ],
        out_specs=pl.BlockSpec((1, SSD_CHUNK, D_MODEL), lambda i, c: (i, fout(c), 0)),
        out_shape=jax.ShapeDtypeStruct((b, s, D_MODEL), BF16),
        scratch_shapes=[state, pltpu.VMEM((SSD_CHUNK, D_MODEL), F32)],
        compiler_params=_params(2, 48),
    )(xs, bmat, cmat, dt, alog, expand_f, y_rev, z, dskip, gain)


LOG2E = 1.4426950408889634
ATT_ONES_ROWS = 16
ATT_DROP_NATS = 40.0


def _attn_kernel(tiles_ref, slopes_ref, q_ref, k_ref, v_ref, kt_ref, vt_ref,
                 lq1_ref, lk1_ref, lq2_ref, lk2_ref, sub_ref, o_ref,
                 vT_ref, vtT_ref, s_ref, st_ref, *, tq, tk, n_kt, n_qt, lam_init):
    h = pl.program_id(1)
    i = pl.program_id(2)
    lo = tiles_ref[2 * (h * n_qt + i)]
    hi = tiles_ref[2 * (h * n_qt + i) + 1]
    slope = slopes_ref[h]
    vrows = ATT_V_DIM + ATT_ONES_ROWS

    @pl.when(i == 0)
    def _():
        for t in range(n_kt):
            vT_ref[t, 0:ATT_V_DIM, :] = v_ref[0, t * tk:(t + 1) * tk, :].astype(F32).T.astype(BF16)
            vT_ref[t, ATT_V_DIM:vrows, :] = jnp.ones((ATT_ONES_ROWS, tk), BF16)
        vtT_ref[0:ATT_V_DIM, :] = vt_ref[...].astype(F32).T.astype(BF16)
        vtT_ref[ATT_V_DIM:vrows, :] = jnp.ones((ATT_ONES_ROWS, LANES), BF16)

    q = q_ref[0]
    lane = lax.broadcasted_iota(jnp.int32, q.shape, 1)
    zero = jnp.zeros_like(q)
    qs = (jnp.where(lane < ATT_HALF_DIM, q, zero), jnp.where(lane >= ATT_HALF_DIM, q, zero))
    rel = (lax.broadcasted_iota(jnp.int32, (tk, tq), 0)
           - lax.broadcasted_iota(jnp.int32, (tk, tq), 1)).astype(F32)

    def scores(k, rel_tile, offset):
        bias = slope * jnp.abs(rel_tile + offset)
        return [lax.dot_general(k, qs[e], (((1,), (1,)), ((), ())),
                                preferred_element_type=F32) - bias for e in range(2)]

    q0 = (i * tq).astype(F32)
    n_tail = kt_ref.shape[0]
    tail_valid = lax.broadcasted_iota(jnp.int32, (n_tail, 1), 0) < N_META
    ms = []
    for e, s in enumerate(scores(kt_ref[...], rel[0:n_tail], -(q0 + N_META))):
        s = jnp.where(tail_valid, s, -jnp.inf)
        st_ref[e] = s
        ms.append(jnp.max(s, axis=0, keepdims=True))

    def pass1(t, ms):
        start = pl.multiple_of(t * tk, tk)
        ss = scores(k_ref[0, pl.ds(start, tk), :], rel, start.astype(F32) - q0)
        out = []
        for e in range(2):
            s_ref[e, pl.ds(start, tk), :] = ss[e]
            out.append(jnp.maximum(ms[e], jnp.max(ss[e], axis=0, keepdims=True)))
        return tuple(out)

    ms = lax.fori_loop(lo, hi, pass1, tuple(ms))

    accs = tuple(
        jnp.dot(vtT_ref[...], jnp.exp2(st_ref[e] - ms[e]).astype(BF16), preferred_element_type=F32)
        for e in range(2))

    def pass2(t, accs):
        start = pl.multiple_of(t * tk, tk)
        vt_aug = vT_ref[t]
        return tuple(
            accs[e] + jnp.dot(vt_aug, jnp.exp2(s_ref[e, pl.ds(start, tk), :] - ms[e]).astype(BF16),
                              preferred_element_type=F32)
            for e in range(2))

    accs = lax.fori_loop(lo, hi, pass2, accs)

    lam = (jnp.exp(jnp.sum(lq1_ref[...] * lk1_ref[...], axis=-1, keepdims=True))
           - jnp.exp(jnp.sum(lq2_ref[...] * lk2_ref[...], axis=-1, keepdims=True))
           + lam_init)
    o1, o2 = (a[0:ATT_V_DIM] / a[ATT_V_DIM:ATT_V_DIM + 1] for a in accs)
    o = o1 - lam * o2
    msq = jnp.mean(o * o, axis=0, keepdims=True)
    o = (o * lax.rsqrt(msq + EPS) * sub_ref[...]) * (1.0 - lam_init)
    o_ref[0] = o.T.astype(o_ref.dtype)


def _attention(q, k, v, k_tail, v_tail, q_gain, k_gain, lambdas, subln_g, layer_idx):
    b, s, _ = q.shape
    tq = min(512, s)
    tk = min(512, s)
    n_qt, n_kt = s // tq, s // tk
    lam_init = 0.8 - 0.6 * math.exp(-0.3 * layer_idx)
    slopes = jnp.exp2(-8.0 * jnp.arange(1, ATT_N_HEADS + 1, dtype=F32) / ATT_N_HEADS)
    bound = 1.01 * ATT_HALF_DIM ** 0.5 * jnp.max(jnp.abs(q_gain)) * jnp.max(jnp.abs(k_gain))
    reach = jnp.ceil((2.0 * bound + ATT_DROP_NATS) / slopes)
    reach = jnp.minimum(reach, float(2 * s)).astype(jnp.int32)[:, None]
    q_first = (jnp.arange(n_qt, dtype=jnp.int32) * tq)[None, :]
    lo = jnp.maximum(q_first - reach, 0) // tk
    hi = jnp.minimum((q_first + tq - 1 + reach) // tk + 1, n_kt)
    tiles = jnp.stack([lo, hi], axis=-1).reshape(-1).astype(jnp.int32)
    kern = functools.partial(_attn_kernel, tq=tq, tk=tk, n_kt=n_kt, n_qt=n_qt, lam_init=lam_init)
    vec = lambda x: x.reshape(1, -1)
    small = lambda n: pl.BlockSpec((1, n), lambda bi, h, i: (0, 0))
    vrows = ATT_V_DIM + ATT_ONES_ROWS
    return pl.pallas_call(
        kern,
        grid=(b, ATT_N_HEADS, n_qt),
        in_specs=[
            pl.BlockSpec(memory_space=pltpu.SMEM),
            pl.BlockSpec(memory_space=pltpu.SMEM),
            pl.BlockSpec((1, tq, LANES), lambda bi, h, i: (bi, i, h)),
            pl.BlockSpec((1, s, LANES), lambda bi, h, i: (bi, 0, h)),
            pl.BlockSpec((1, s, LANES), lambda bi, h, i: (bi, 0, h)),
            pl.BlockSpec((LANES, LANES), lambda bi, h, i: (0, h)),
            pl.BlockSpec((LANES, LANES), lambda bi, h, i: (0, h)),
            small(ATT_HALF_DIM), small(ATT_HALF_DIM), small(ATT_HALF_DIM), small(ATT_HALF_DIM),
            pl.BlockSpec((ATT_V_DIM, 1), lambda bi, h, i: (0, 0)),
        ],
        out_specs=pl.BlockSpec((1, tq, LANES), lambda bi, h, i: (bi, i, h)),
        out_shape=jax.ShapeDtypeStruct((b, s, ATT_WIDTH), BF16),
        scratch_shapes=[
            pltpu.VMEM((n_kt, vrows, tk), BF16),
            pltpu.VMEM((vrows, LANES), BF16),
            pltpu.VMEM((2, s, tq), F32),
            pltpu.VMEM((2, LANES, tq), F32),
        ],
        compiler_params=_params(3, 48),
    )(tiles, slopes * LOG2E, q, k, v, k_tail, v_tail, *[vec(x) for x in lambdas],
      subln_g.reshape(-1, 1))


def _proj(h_in, w, n, dtype, ep=None, **kw):
    return _matmul([h_in], [(w, 0)], [(0, 0)], ep or _ep_plain, n, dtype, 1024, min(1024, n), 52, **kw)


def _prepare(meta_tokens, wts):
    bf = lambda w: w.astype(BF16)
    o1 = D_MODEL
    o2 = o1 + SSD_CONV_CH
    o3 = o2 + 2 * SSD_N_HEADS
    o4 = o3 + ATT_WIDTH
    o5 = o4 + ATT_WIDTH
    w_in = wts["w_in"]
    p = dict(wts)
    p["w_z"], p["w_xbc"], p["w_dt"] = bf(w_in[:, :o1]), bf(w_in[:, o1:o2]), bf(w_in[:, o2:o3])
    p["w_q"], p["w_k"], p["w_v"] = bf(w_in[:, o3:o4]), bf(w_in[:, o4:o5]), bf(w_in[:, o5:])
    for name in ("w_branch_gate", "w_ssd_out", "w_att_out", "w_o", "w_gate_up", "w_down"):
        p[name] = bf(wts[name])
    ones_bd = (jnp.arange(256)[:, None] // ATT_HALF_DIM
               == jnp.arange(256)[None, :] // ATT_HALF_DIM).astype(BF16)
    p["qk_norm"] = dict(
        q=dict(rows=[(jnp.tile(wts["q_norm_g"] * (ATT_HALF_DIM ** -0.5 * LOG2E),
                               2 * ATT_N_HEADS).reshape(1, ATT_WIDTH), 0)], consts=[ones_bd]),
        k=dict(rows=[(jnp.tile(wts["k_norm_g"], 2 * ATT_N_HEADS).reshape(1, ATT_WIDTH), 0)],
               consts=[ones_bd]))
    hn_meta = _rmsnorm(meta_tokens, wts["norm_mix_g"], N_META)
    tail_pad = ((0, LANES - N_META), (0, 0))
    p["xbc_meta"] = _proj(hn_meta, p["w_xbc"], SSD_CONV_CH, BF16)
    p["dt_raw_meta"] = _proj(hn_meta, p["w_dt"], 2 * SSD_N_HEADS, F32)
    p["k_tail"] = jnp.pad(_proj(hn_meta, p["w_k"], ATT_WIDTH, BF16, ep=_ep_qknorm,
                                **p["qk_norm"]["k"]), tail_pad)
    p["v_tail"] = jnp.pad(_proj(hn_meta, p["w_v"], ATT_WIDTH, BF16), tail_pad)
    return p


def _layer(x, layer_idx, p):
    b, s, d = x.shape
    m = b * s
    x2 = x.reshape(m, d)
    hn = _rmsnorm(x2, p["norm_mix_g"], 256)

    z = _proj(hn, p["w_z"], D_MODEL, BF16)
    xbc = _proj(hn, p["w_xbc"], SSD_CONV_CH, BF16)
    dt_raw = _proj(hn, p["w_dt"], 2 * SSD_N_HEADS, F32)
    q = _proj(hn, p["w_q"], ATT_WIDTH, BF16, ep=_ep_qknorm, **p["qk_norm"]["q"])
    k = _proj(hn, p["w_k"], ATT_WIDTH, BF16, ep=_ep_qknorm, **p["qk_norm"]["k"])
    v = _proj(hn, p["w_v"], ATT_WIDTH, BF16)

    xs, bmat, cmat, dt = _conv_dt(
        xbc.reshape(b, s, SSD_CONV_CH), p["xbc_meta"], p["conv_w"], p["conv_b"],
        dt_raw.reshape(b, s, 2 * SSD_N_HEADS), p["dt_raw_meta"], p["dt_bias"].reshape(-1))
    y_ssd = _ssd(xs, bmat, cmat, dt, p["a_log"].reshape(-1), p["d_skip"],
                 p["ssd_norm_g"], z.reshape(b, s, D_MODEL))

    y_att = _attention(
        q.reshape(b, s, ATT_WIDTH), k.reshape(b, s, ATT_WIDTH), v.reshape(b, s, ATT_WIDTH),
        p["k_tail"], p["v_tail"], p["q_norm_g"], p["k_norm_g"],
        (p["lambda_q1"], p["lambda_k1"], p["lambda_q2"], p["lambda_k2"]),
        p["subln_g"], layer_idx)

    gates = _matmul([hn], [(p["w_branch_gate"], 0)], [(0, 0)], _ep_sigmoid_bias,
                    2 * D_MODEL, BF16, 1024, 1024, 52,
                    rows=[(p["b_branch_gate"].reshape(1, -1), 0)])
    bn_mix = 512
    mixed = _matmul([y_ssd.reshape(m, D_MODEL), y_att.reshape(m, ATT_WIDTH)],
                    [(p["w_ssd_out"], 0), (p["w_att_out"], 0)],
                    [(0, 0), (1, 1)], _ep_gated_merge, D_MODEL, BF16, 1024, bn_mix, 56,
                    tiles=[(gates, 0), (gates, D_MODEL // bn_mix)])
    h2 = _matmul([mixed], [(p["w_o"], 0)], [(0, 0)], _ep_residual,
                 D_MODEL, F32, 1024, 512, 48, tiles=[(x2, 0)])

    hf = _rmsnorm(h2, p["norm_ffn_g"], 256)
    bn_ff = 256
    act = _matmul([hf], [(p["w_gate_up"], 0), (p["w_gate_up"], D_FF // bn_ff)],
                  [(0, 0), (0, 1)], _ep_swiglu, D_FF, BF16, 1024, bn_ff, 40)
    out = _matmul([act], [(p["w_down"], 0)], [(0, 0)], _ep_residual,
                  D_MODEL, F32, 512, 256, 48, tiles=[(h2, 0)])
    return out.reshape(b, s, d)


def kernel(x_prompt, x_sample, meta_tokens, norm_mix_g, w_in, conv_w, conv_b, dt_bias, a_log, d_skip, ssd_norm_g, q_norm_g, k_norm_g, lambda_q1, lambda_k1, lambda_q2, lambda_k2, subln_g, w_branch_gate, b_branch_gate, w_ssd_out, w_att_out, w_o, norm_ffn_g, w_gate_up, w_down):
    names = ("norm_mix_g", "w_in", "conv_w", "conv_b", "dt_bias", "a_log", "d_skip",
             "ssd_norm_g", "q_norm_g", "k_norm_g", "lambda_q1", "lambda_k1", "lambda_q2",
             "lambda_k2", "subln_g", "w_branch_gate", "b_branch_gate", "w_ssd_out",
             "w_att_out", "w_o", "norm_ffn_g", "w_gate_up", "w_down")
    stacked = (norm_mix_g, w_in, conv_w, conv_b, dt_bias, a_log, d_skip, ssd_norm_g,
               q_norm_g, k_norm_g, lambda_q1, lambda_k1, lambda_q2, lambda_k2, subln_g,
               w_branch_gate, b_branch_gate, w_ssd_out, w_att_out, w_o, norm_ffn_g,
               w_gate_up, w_down)
    assert all(w.shape[0] == 1 for w in stacked), "single-layer encoder"
    p = _prepare(meta_tokens, {n: w[0] for n, w in zip(names, stacked)})
    return (_layer(x_prompt, 0, p), _layer(x_sample, 0, p))
```

```python
import functools
import math

import jax
import jax.numpy as jnp
from jax import lax
from jax.experimental import pallas as pl
from jax.experimental.pallas import tpu as pltpu

F32 = jnp.float32
BF16 = jnp.bfloat16

D_MODEL = 4096
N_META = 16
EPS = 1e-6
SSD_HEAD_DIM = 64
SSD_N_HEADS = D_MODEL // SSD_HEAD_DIM
SSD_N_GROUPS = 8
SSD_HEADS_PER_GROUP = SSD_N_HEADS // SSD_N_GROUPS
SSD_D_STATE = 128
SSD_CONV = 5
SSD_CHUNK = 128
SSD_GN = SSD_N_GROUPS * SSD_D_STATE
SSD_CONV_CH = D_MODEL + 2 * SSD_GN
SSD_GROUP_W = SSD_HEADS_PER_GROUP * SSD_HEAD_DIM
ATT_N_HEADS = 16
ATT_HALF_DIM = 64
ATT_V_DIM = 128
ATT_WIDTH = ATT_N_HEADS * ATT_V_DIM
D_FF = 11008
LANES = 128
HALO = 16
MIB = 1024 * 1024


def _params(n_axes, vmem_mib):
    return pltpu.CompilerParams(
        dimension_semantics=("arbitrary",) * n_axes,
        vmem_limit_bytes=vmem_mib * MIB)


def _rmsnorm_kernel(x_ref, g_ref, o_ref):
    x = x_ref[...]
    ms = jnp.mean(x * x, axis=-1, keepdims=True)
    o_ref[...] = (x * lax.rsqrt(ms + EPS) * g_ref[...]).astype(o_ref.dtype)


def _rmsnorm(x, g, bm):
    m, d = x.shape
    bm = min(bm, m)
    return pl.pallas_call(
        _rmsnorm_kernel,
        grid=(m // bm,),
        in_specs=[pl.BlockSpec((bm, d), lambda i: (i, 0)),
                  pl.BlockSpec((1, d), lambda i: (0, 0))],
        out_specs=pl.BlockSpec((bm, d), lambda i: (i, 0)),
        out_shape=jax.ShapeDtypeStruct((m, d), BF16),
        compiler_params=_params(1, 32),
    )(x, g.reshape(1, d))


def _mm_kernel(*refs, dots, n_lhs, n_rhs, n_tiles, n_rows, n_consts, epilogue):
    lhs = refs[:n_lhs]
    rhs = refs[n_lhs:n_lhs + n_rhs]
    p = n_lhs + n_rhs
    tiles = refs[p:p + n_tiles]
    p += n_tiles
    rows = refs[p:p + n_rows]
    p += n_rows
    consts = refs[p:p + n_consts]
    o_ref = refs[-1]
    accs = [jnp.dot(lhs[a][...], rhs[b][...], preferred_element_type=F32)
            for a, b in dots]
    out = epilogue(accs, [t[...] for t in tiles], [r[...] for r in rows],
                   [c[...] for c in consts])
    o_ref[...] = out.astype(o_ref.dtype)


def _matmul(lhs, rhs, dots, epilogue, n_out, out_dtype, bm, bn, vmem_mib,
            tiles=(), rows=(), consts=()):
    m = lhs[0].shape[0]
    bm = min(bm, m)
    in_specs = []
    args = []
    for a in lhs:
        in_specs.append(pl.BlockSpec((bm, a.shape[1]), lambda i, j: (i, 0)))
        args.append(a)
    for w, off in rhs:
        in_specs.append(pl.BlockSpec((w.shape[0], bn), lambda i, j, off=off: (0, j + off)))
        args.append(w)
    for t, off in tiles:
        in_specs.append(pl.BlockSpec((bm, bn), lambda i, j, off=off: (i, j + off)))
        args.append(t)
    for r, off in rows:
        in_specs.append(pl.BlockSpec((1, bn), lambda i, j, off=off: (0, j + off)))
        args.append(r)
    for c in consts:
        in_specs.append(pl.BlockSpec(c.shape, lambda i, j, nd=c.ndim: (0,) * nd))
        args.append(c)
    kern = functools.partial(
        _mm_kernel, dots=tuple(dots), n_lhs=len(lhs), n_rhs=len(rhs),
        n_tiles=len(tiles), n_rows=len(rows), n_consts=len(consts), epilogue=epilogue)
    return pl.pallas_call(
        kern,
        grid=(m // bm, n_out // bn),
        in_specs=in_specs,
        out_specs=pl.BlockSpec((bm, bn), lambda i, j: (i, j)),
        out_shape=jax.ShapeDtypeStruct((m, n_out), out_dtype),
        compiler_params=_params(2, vmem_mib),
    )(*args)


def _ep_plain(accs, tiles, rows, consts):
    return accs[0]


def _ep_qknorm(accs, tiles, rows, consts):
    acc = accs[0]
    ones_bd = consts[0]
    width = ones_bd.shape[0]
    outs = []
    for c in range(acc.shape[1] // width):
        a = acc[:, c * width:(c + 1) * width]
        ss = jnp.dot((a * a).astype(BF16), ones_bd, preferred_element_type=F32)
        outs.append(a * lax.rsqrt(ss * (1.0 / ATT_HALF_DIM) + EPS))
    return jnp.concatenate(outs, axis=1) * rows[0]


def _ep_sigmoid_bias(accs, tiles, rows, consts):
    return 1.0 / (1.0 + jnp.exp(-(accs[0] + rows[0])))


def _ep_gated_merge(accs, tiles, rows, consts):
    return tiles[0].astype(F32) * accs[0] + tiles[1].astype(F32) * accs[1]


def _ep_residual(accs, tiles, rows, consts):
    return tiles[0] + accs[0]


def _ep_swiglu(accs, tiles, rows, consts):
    g = accs[0]
    return (g * (1.0 / (1.0 + jnp.exp(-g)))) * accs[1]


def _conv_kernel(cur_ref, prev_ref, next_ref, meta_ref, w_ref, b_ref, shift_ref,
                 dtr_ref, dtm_ref, dtb_ref,
                 xs_ref, bm_ref, cm_ref, dt_ref, ext_ref, *, n_chunks):
    c = pl.program_id(1)
    pad = SSD_CHUNK - N_META
    ch = ext_ref.shape[1]
    dtype = ext_ref.dtype

    @pl.when(c == 0)
    def _():
        ext_ref[0:HALO + pad, :] = jnp.zeros((HALO + pad, ch), dtype)
        ext_ref[HALO + pad:HALO + SSD_CHUNK, :] = meta_ref[...]

    @pl.when(c == 1)
    def _():
        ext_ref[0:HALO, :] = meta_ref[...]

    @pl.when(c > 1)
    def _():
        ext_ref[0:HALO, :] = prev_ref[0]

    @pl.when(c > 0)
    def _():
        ext_ref[HALO:HALO + SSD_CHUNK, :] = cur_ref[0]

    @pl.when(c < n_chunks - 1)
    def _():
        ext_ref[HALO + SSD_CHUNK:, :] = next_ref[0]

    @pl.when(c == n_chunks - 1)
    def _():
        ext_ref[HALO + SSD_CHUNK:, :] = jnp.zeros((HALO, ch), dtype)

    half = (SSD_CONV - 1) // 2

    def conv_silu(lo, hi):
        ext = ext_ref[:, lo:hi]
        shifted = jnp.dot(shift_ref[...], ext, preferred_element_type=F32)
        acc = None
        idx = 0
        for k in range(SSD_CONV):
            if k == half:
                x = ext[HALO:HALO + SSD_CHUNK].astype(F32)
            else:
                x = shifted[idx * SSD_CHUNK:(idx + 1) * SSD_CHUNK]
                idx += 1
            term = x * w_ref[k:k + 1, lo:hi]
            acc = term if acc is None else acc + term
        acc = acc + b_ref[:, lo:hi]
        return acc * (1.0 / (1.0 + jnp.exp(-acc)))

    blk = 1024
    for s in range(D_MODEL // blk):
        xs_ref[0, :, s * blk:(s + 1) * blk] = conv_silu(s * blk, (s + 1) * blk).astype(xs_ref.dtype)
    bm_ref[0] = conv_silu(D_MODEL, D_MODEL + SSD_GN).astype(bm_ref.dtype)
    cm_ref[0] = conv_silu(D_MODEL + SSD_GN, SSD_CONV_CH).astype(cm_ref.dtype)

    @pl.when(c == 0)
    def _():
        xs_ref[0, 0:pad, :] = jnp.zeros((pad, D_MODEL), xs_ref.dtype)
        bm_ref[0, 0:pad, :] = jnp.zeros((pad, SSD_GN), bm_ref.dtype)
        cm_ref[0, 0:pad, :] = jnp.zeros((pad, SSD_GN), cm_ref.dtype)

    row = lax.broadcasted_iota(jnp.int32, (SSD_CHUNK, dt_ref.shape[2]), 0)
    valid = jnp.logical_or(c > 0, row >= pad)

    raw_real = dtr_ref[0]
    raw_meta = jnp.concatenate(
        [jnp.zeros((pad, raw_real.shape[1]), F32), dtm_ref[...]], axis=0)
    raw = jnp.where(c > 0, raw_real, raw_meta) + dtb_ref[...]
    sp = jnp.maximum(raw, 0.0) + jnp.log1p(jnp.exp(-jnp.abs(raw)))
    dt_ref[0] = jnp.where(valid, sp, 0.0)


def _conv_dt(xbc, xbc_meta, conv_w, conv_b, dt_raw, dt_raw_meta, dt_bias):
    b, s, ch = xbc.shape
    n_chunks = s // SSD_CHUNK + 1
    lp = n_chunks * SSD_CHUNK
    hb = SSD_CHUNK // HALO
    n_halo = s // HALO
    nd = dt_raw.shape[-1]
    kern = functools.partial(_conv_kernel, n_chunks=n_chunks)
    n_ext = 2 * HALO + SSD_CHUNK
    half = (SSD_CONV - 1) // 2
    taps = jnp.array([k - half for k in range(SSD_CONV) if k != half], jnp.int32)
    src = HALO + jnp.arange(SSD_CHUNK, dtype=jnp.int32)[None, :] + taps[:, None]
    shift = (src.reshape(-1, 1) == jnp.arange(n_ext, dtype=jnp.int32)[None, :]).astype(BF16)
    return pl.pallas_call(
        kern,
        grid=(b, n_chunks),
        in_specs=[
            pl.BlockSpec((1, SSD_CHUNK, ch), lambda i, c: (i, jnp.maximum(c - 1, 0), 0)),
            pl.BlockSpec((1, HALO, ch), lambda i, c: (i, jnp.maximum((c - 1) * hb - 1, 0), 0)),
            pl.BlockSpec((1, HALO, ch), lambda i, c: (i, jnp.minimum(c * hb, n_halo - 1), 0)),
            pl.BlockSpec((N_META, ch), lambda i, c: (0, 0)),
            pl.BlockSpec((SSD_CONV, ch), lambda i, c: (0, 0)),
            pl.BlockSpec((1, ch), lambda i, c: (0, 0)),
            pl.BlockSpec(((SSD_CONV - 1) * SSD_CHUNK, n_ext), lambda i, c: (0, 0)),
            pl.BlockSpec((1, SSD_CHUNK, nd), lambda i, c: (i, jnp.maximum(c - 1, 0), 0)),
            pl.BlockSpec((N_META, nd), lambda i, c: (0, 0)),
            pl.BlockSpec((1, nd), lambda i, c: (0, 0)),
        ],
        out_specs=[
            pl.BlockSpec((1, SSD_CHUNK, D_MODEL), lambda i, c: (i, c, 0)),
            pl.BlockSpec((1, SSD_CHUNK, SSD_GN), lambda i, c: (i, c, 0)),
            pl.BlockSpec((1, SSD_CHUNK, SSD_GN), lambda i, c: (i, c, 0)),
            pl.BlockSpec((1, SSD_CHUNK, nd), lambda i, c: (i, c, 0)),
        ],
        out_shape=[
            jax.ShapeDtypeStruct((b, lp, D_MODEL), BF16),
            jax.ShapeDtypeStruct((b, lp, SSD_GN), BF16),
            jax.ShapeDtypeStruct((b, lp, SSD_GN), BF16),
            jax.ShapeDtypeStruct((b, lp, nd), F32),
        ],
        scratch_shapes=[pltpu.VMEM((n_ext, ch), BF16)],
        compiler_params=_params(2, 40),
    )(xbc, xbc, xbc, xbc_meta, conv_w, conv_b.reshape(1, ch), shift,
      dt_raw, dt_raw_meta, dt_bias.reshape(1, nd))


def _split3(x):
    hi = x.astype(BF16)
    r = x - hi.astype(F32)
    mid = r.astype(BF16)
    lo = (r - mid.astype(F32)).astype(BF16)
    return hi, mid, lo


def _ssd_body(x_ref, b_ref, c_ref, dt_ref, alog_ref, exp_ref, state_ref, *, reverse):
    t = SSD_CHUNK
    lane0 = SSD_N_HEADS if reverse else 0
    dt = dt_ref[0]
    a = dt * (-jnp.exp(alog_ref[...]))
    ri = lax.broadcasted_iota(jnp.int32, (t, t), 0)
    ci = lax.broadcasted_iota(jnp.int32, (t, t), 1)
    causal = (ci >= ri) if reverse else (ci <= ri)
    tri = causal.astype(BF16)
    a3 = _split3(a)
    cs = sum(jnp.dot(tri, p, preferred_element_type=F32) for p in a3)
    cs_t = cs.T
    last = t - 1
    edge = 0 if reverse else last
    cs_edge = cs[edge:edge + 1, :]
    decay_s = jnp.exp(cs_edge - cs)
    ecs = jnp.exp(cs)
    edge_parts = jnp.concatenate(
        [p.astype(F32) for p in _split3(ecs[edge:edge + 1, :])]
        + [jnp.zeros((HALO - 3, LANES), F32)], axis=0)
    stacked = jnp.concatenate([dt, decay_s, ecs, edge_parts], axis=0).astype(BF16)
    expanded = jnp.dot(stacked, exp_ref[...], preferred_element_type=F32)
    chunk_decay_all = (expanded[3 * t:3 * t + 1] + expanded[3 * t + 1:3 * t + 2]
                       + expanded[3 * t + 2:3 * t + 3])
    lane = lax.broadcasted_iota(jnp.int32, (t, LANES), 1)
    first_half = lane < SSD_HEAD_DIM
    outs = []
    for g in range(SSD_N_GROUPS):
        gs = slice(g * SSD_GROUP_W, (g + 1) * SSD_GROUP_W)
        ns = slice(g * SSD_D_STATE, (g + 1) * SSD_D_STATE)
        bg = b_ref[0, :, ns]
        cg = c_ref[0, :, ns]
        cb = lax.dot_general(cg, bg, (((1,), (1,)), ((), ())), preferred_element_type=F32)
        xdt = x_ref[0, :, gs].astype(F32) * expanded[0:t, gs]
        xdt_bf = xdt.astype(BF16)
        xdtd = (xdt * expanded[t:2 * t, gs]).astype(BF16)
        new_state = lax.dot_general(bg, xdtd, (((0,), (0,)), ((), ())),
                                    preferred_element_type=F32)
        prev = state_ref[g]
        y_off = jnp.dot(cg, prev.astype(BF16), preferred_element_type=F32) * expanded[2 * t:3 * t, gs]
        state_ref[g] = prev * chunk_decay_all[:, gs] + new_state
        for jp in range(SSD_HEADS_PER_GROUP // 2):
            ys = []
            xpair = xdt_bf[:, jp * LANES:(jp + 1) * LANES]
            for e in range(2):
                hl = lane0 + g * SSD_HEADS_PER_GROUP + 2 * jp + e
                col = cs[:, hl:hl + 1]
                rowv = cs_t[hl:hl + 1, :]
                seg = jnp.where(causal, col - rowv, 0.0)
                mh = jnp.where(causal, cb * jnp.exp(seg), 0.0).astype(BF16)
                ys.append(jnp.dot(mh, xpair, preferred_element_type=F32))
            y_pair = jnp.where(first_half, ys[0], ys[1]) + y_off[:, jp * LANES:(jp + 1) * LANES]
            outs.append(y_pair)
    return outs


def _ssd_rev_kernel(x_ref, b_ref, c_ref, dt_ref, alog_ref, exp_ref, y_ref, state_ref, *, n_chunks):
    step = pl.program_id(1)

    @pl.when(step == 0)
    def _():
        state_ref[...] = jnp.zeros(state_ref.shape, F32)

    outs = _ssd_body(x_ref, b_ref, c_ref, dt_ref, alog_ref, exp_ref, state_ref, reverse=True)

    @pl.when(step < n_chunks - 1)
    def _():
        for k, y in enumerate(outs):
            y_ref[0, :, k * LANES:(k + 1) * LANES] = y


def _ssd_fwd_kernel(x_ref, b_ref, c_ref, dt_ref, alog_ref, exp_ref, yb_ref, z_ref,
                    dskip_ref, gain_ref, o_ref, state_ref, ybuf_ref):
    step = pl.program_id(1)

    @pl.when(step == 0)
    def _():
        state_ref[...] = jnp.zeros(state_ref.shape, F32)

    outs = _ssd_body(x_ref, b_ref, c_ref, dt_ref, alog_ref, exp_ref, state_ref, reverse=False)

    @pl.when(step > 0)
    def _():
        for k, y in enumerate(outs):
            ls = slice(k * LANES, (k + 1) * LANES)
            x = x_ref[0, :, ls].astype(F32)
            z = z_ref[0, :, ls].astype(F32)
            yk = (y + yb_ref[0, :, ls]) + x * dskip_ref[:, ls]
            ybuf_ref[:, ls] = yk * (z * (1.0 / (1.0 + jnp.exp(-z))))
        for g in range(SSD_N_GROUPS):
            gs = slice(g * SSD_GROUP_W, (g + 1) * SSD_GROUP_W)
            yg = ybuf_ref[:, gs]
            ms = jnp.mean(yg * yg, axis=-1, keepdims=True)
            o_ref[0, :, gs] = (yg * lax.rsqrt(ms + EPS) * gain_ref[:, gs]).astype(o_ref.dtype)


def _ssd(xs, bmat, cmat, dt, a_log, d_skip, ssd_norm_g, z):
    b, lp, _ = xs.shape
    n_chunks = lp // SSD_CHUNK
    s = lp - SSD_CHUNK
    nd = dt.shape[-1]
    alog = a_log.reshape(1, nd)
    heads = jnp.arange(D_MODEL, dtype=jnp.int32) // SSD_HEAD_DIM
    lanes = jnp.arange(nd, dtype=jnp.int32)
    expand_f = (lanes[:, None] == heads[None, :]).astype(BF16)
    expand_r = (lanes[:, None] == heads[None, :] + SSD_N_HEADS).astype(BF16)
    dskip = jnp.repeat(d_skip, SSD_HEAD_DIM).reshape(1, D_MODEL)
    gain = ssd_norm_g.reshape(1, D_MODEL)
    state = pltpu.VMEM((SSD_N_GROUPS, SSD_D_STATE, SSD_GROUP_W), F32)

    def chunk_specs(cmap):
        return [
            pl.BlockSpec((1, SSD_CHUNK, D_MODEL), lambda i, c: (i, cmap(c), 0)),
            pl.BlockSpec((1, SSD_CHUNK, SSD_GN), lambda i, c: (i, cmap(c), 0)),
            pl.BlockSpec((1, SSD_CHUNK, SSD_GN), lambda i, c: (i, cmap(c), 0)),
            pl.BlockSpec((1, SSD_CHUNK, nd), lambda i, c: (i, cmap(c), 0)),
            pl.BlockSpec((1, nd), lambda i, c: (0, 0)),
            pl.BlockSpec((nd, D_MODEL), lambda i, c: (0, 0)),
        ]

    rmap = lambda c: n_chunks - 1 - c
    rout = lambda c: jnp.maximum(n_chunks - 2 - c, 0)
    y_rev = pl.pallas_call(
        functools.partial(_ssd_rev_kernel, n_chunks=n_chunks),
        grid=(b, n_chunks),
        in_specs=chunk_specs(rmap),
        out_specs=pl.BlockSpec((1, SSD_CHUNK, D_MODEL), lambda i, c: (i, rout(c), 0)),
        out_shape=jax.ShapeDtypeStruct((b, s, D_MODEL), F32),
        scratch_shapes=[state],
        compiler_params=_params(2, 48),
    )(xs, bmat, cmat, dt, alog, expand_r)

    fmap = lambda c: c
    fout = lambda c: jnp.maximum(c - 1, 0)
    return pl.pallas_call(
        _ssd_fwd_kernel,
        grid=(b, n_chunks),
        in_specs=chunk_specs(fmap) + [
            pl.BlockSpec((1, SSD_CHUNK, D_MODEL), lambda i, c: (i, fout(c), 0)),
            pl.BlockSpec((1, SSD_CHUNK, D_MODEL), lambda i, c: (i, fout(c), 0)),
            pl.BlockSpec((1, D_MODEL), lambda i, c: (0, 0)),
            pl.BlockSpec((1, D_MODEL), lambda i, c: (0, 0)),
        ],
        out_specs=pl.BlockSpec((1, SSD_CHUNK, D_MODEL), lambda i, c: (i, fout(c), 0)),
        out_shape=jax.ShapeDtypeStruct((b, s, D_MODEL), BF16),
        scratch_shapes=[state, pltpu.VMEM((SSD_CHUNK, D_MODEL), F32)],
        compiler_params=_params(2, 48),
    )(xs, bmat, cmat, dt, alog, expand_f, y_rev, z, dskip, gain)


LOG2E = 1.4426950408889634
ATT_ONES_ROWS = 16
ATT_DROP_NATS = 40.0


def _attn_kernel(tiles_ref, slopes_ref, q_ref, k_ref, v_ref, kt_ref, vt_ref,
                 lq1_ref, lk1_ref, lq2_ref, lk2_ref, sub_ref, o_ref,
                 vT_ref, vtT_ref, bias_ref, s0_ref, s1_ref, acc_ref, *, tq, tk, n_kt, n_qt, lam_init):
    h = pl.program_id(1)
    i = pl.program_id(2)
    lo = tiles_ref[2 * (h * n_qt + i)]
    hi = tiles_ref[2 * (h * n_qt + i) + 1]
    slope = slopes_ref[h]
    vrows = ATT_V_DIM + ATT_ONES_ROWS

    @pl.when(i == 0)
    def _():
        for t in range(n_kt):
            vT_ref[t, 0:ATT_V_DIM, :] = v_ref[0, t * tk:(t + 1) * tk, :].astype(F32).T.astype(BF16)
            vT_ref[t, ATT_V_DIM:vrows, :] = jnp.ones((ATT_ONES_ROWS, tk), BF16)
        vtT_ref[0:ATT_V_DIM, :] = vt_ref[...].astype(F32).T.astype(BF16)
        vtT_ref[ATT_V_DIM:vrows, :] = jnp.ones((ATT_ONES_ROWS, LANES), BF16)
        srel = slope * (lax.broadcasted_iota(jnp.int32, (tk, tq), 0)
                        - lax.broadcasted_iota(jnp.int32, (tk, tq), 1)).astype(F32)
        bias_ref[0] = -srel
        bias_ref[1] = srel
        bias_ref[2] = jnp.abs(srel)

    q = q_ref[0]
    lane = lax.broadcasted_iota(jnp.int32, q.shape, 1)
    zero = jnp.zeros_like(q)
    qs = (jnp.where(lane < ATT_HALF_DIM, q, zero), jnp.where(lane >= ATT_HALF_DIM, q, zero))

    def scores(k, bias_tile):
        return [lax.dot_general(k, qs[e], (((1,), (1,)), ((), ())),
                                preferred_element_type=F32) - bias_tile for e in range(2)]

    def probs(s, m):
        return jnp.exp2(s - m).astype(BF16)

    q0 = (i * tq).astype(F32)
    n_tail = kt_ref.shape[0]
    tail_valid = lax.broadcasted_iota(jnp.int32, (n_tail, 1), 0) < N_META
    c_tail = slope * (q0 + N_META)
    ms = []
    for e, s in enumerate(scores(kt_ref[...], bias_ref[0, 0:n_tail, :])):
        s = jnp.where(tail_valid, s, -jnp.inf)
        m = jnp.max(s, axis=0, keepdims=True)
        acc_ref[e] = jnp.dot(vtT_ref[...], probs(s, m), preferred_element_type=F32)
        ms.append(m - c_tail)

    def stage_a(t, dst_ref):
        start = pl.multiple_of(t * tk, tk)
        side = jnp.where(t < i, 0, jnp.where(t > i, 1, 2))
        c = slope * jnp.abs(start.astype(F32) - q0)
        ss = scores(k_ref[0, pl.ds(start, tk), :], bias_ref[side])
        for e in range(2):
            dst_ref[e] = ss[e]
        return tuple(jnp.max(ss[e], axis=0, keepdims=True) - c for e in range(2)) + (c,)

    def stage_b(t, src_ref, ms, cmax):
        vt_aug = vT_ref[t]
        out = []
        for e in range(2):
            m_new = jnp.maximum(ms[e], cmax[e])
            alpha = jnp.exp2(ms[e] - m_new)
            p = probs(src_ref[e], m_new + cmax[2])
            acc_ref[e] = alpha * acc_ref[e] + jnp.dot(vt_aug, p, preferred_element_type=F32)
            out.append(m_new)
        return tuple(out)

    def two_trips(u, carry):
        ms, cmax0 = carry
        t = lo + 2 * u
        cmax1 = stage_a(t + 1, s1_ref)
        ms = stage_b(t, s0_ref, ms, cmax0)
        cmax0 = stage_a(t + 2, s0_ref)
        ms = stage_b(t + 1, s1_ref, ms, cmax1)
        return ms, cmax0

    n_tiles = hi - lo
    n_two = (n_tiles - 1) // 2
    ms, cmax0 = lax.fori_loop(0, n_two, two_trips, (tuple(ms), stage_a(lo, s0_ref)))
    t_last = lo + 2 * n_two

    def finish_two(ms, cmax0):
        cmax1 = stage_a(t_last + 1, s1_ref)
        stage_b(t_last + 1, s1_ref, stage_b(t_last, s0_ref, ms, cmax0), cmax1)

    def finish_one(ms, cmax0):
        stage_b(t_last, s0_ref, ms, cmax0)

    lax.cond(n_tiles - 2 * n_two == 2, finish_two, finish_one, ms, cmax0)
    accs = (acc_ref[0], acc_ref[1])

    lam = (jnp.exp(jnp.sum(lq1_ref[...] * lk1_ref[...], axis=-1, keepdims=True))
           - jnp.exp(jnp.sum(lq2_ref[...] * lk2_ref[...], axis=-1, keepdims=True))
           + lam_init)
    o1, o2 = (a[0:ATT_V_DIM] / a[ATT_V_DIM:ATT_V_DIM + 1] for a in accs)
    o = o1 - lam * o2
    msq = jnp.mean(o * o, axis=0, keepdims=True)
    o = (o * lax.rsqrt(msq + EPS) * sub_ref[...]) * (1.0 - lam_init)
    o_ref[0] = o.T.astype(o_ref.dtype)


def _attention(q, k, v, k_tail, v_tail, q_gain, k_gain, lambdas, subln_g, layer_idx):
    b, s, _ = q.shape
    tq = tk = min(512, s)
    n_qt, n_kt = s // tq, s // tk
    lam_init = 0.8 - 0.6 * math.exp(-0.3 * layer_idx)
    slopes = jnp.exp2(-8.0 * jnp.arange(1, ATT_N_HEADS + 1, dtype=F32) / ATT_N_HEADS)
    bound = 1.01 * ATT_HALF_DIM ** 0.5 * jnp.max(jnp.abs(q_gain)) * jnp.max(jnp.abs(k_gain))
    reach = jnp.ceil((2.0 * bound + ATT_DROP_NATS) / slopes)
    reach = jnp.minimum(reach, float(2 * s)).astype(jnp.int32)[:, None]
    q_first = (jnp.arange(n_qt, dtype=jnp.int32) * tq)[None, :]
    lo = jnp.maximum(q_first - reach, 0) // tk
    hi = jnp.minimum((q_first + tq - 1 + reach) // tk + 1, n_kt)
    tiles = jnp.stack([lo, hi], axis=-1).reshape(-1).astype(jnp.int32)
    kern = functools.partial(_attn_kernel, tq=tq, tk=tk, n_kt=n_kt, n_qt=n_qt, lam_init=lam_init)
    vec = lambda x: x.reshape(1, -1)
    small = lambda n: pl.BlockSpec((1, n), lambda bi, h, i: (0, 0))
    vrows = ATT_V_DIM + ATT_ONES_ROWS
    return pl.pallas_call(
        kern,
        grid=(b, ATT_N_HEADS, n_qt),
        in_specs=[
            pl.BlockSpec(memory_space=pltpu.SMEM),
            pl.BlockSpec(memory_space=pltpu.SMEM),
            pl.BlockSpec((1, tq, LANES), lambda bi, h, i: (bi, i, h)),
            pl.BlockSpec((1, s, LANES), lambda bi, h, i: (bi, 0, h)),
            pl.BlockSpec((1, s, LANES), lambda bi, h, i: (bi, 0, h)),
            pl.BlockSpec((LANES, LANES), lambda bi, h, i: (0, h)),
            pl.BlockSpec((LANES, LANES), lambda bi, h, i: (0, h)),
            small(ATT_HALF_DIM), small(ATT_HALF_DIM), small(ATT_HALF_DIM), small(ATT_HALF_DIM),
            pl.BlockSpec((ATT_V_DIM, 1), lambda bi, h, i: (0, 0)),
        ],
        out_specs=pl.BlockSpec((1, tq, LANES), lambda bi, h, i: (bi, i, h)),
        out_shape=jax.ShapeDtypeStruct((b, s, ATT_WIDTH), BF16),
        scratch_shapes=[
            pltpu.VMEM((n_kt, vrows, tk), BF16),
            pltpu.VMEM((vrows, LANES), BF16),
            pltpu.VMEM((3, tk, tq), F32),
            pltpu.VMEM((2, tk, tq), F32),
            pltpu.VMEM((2, tk, tq), F32),
            pltpu.VMEM((2, vrows, tq), F32),
        ],
        compiler_params=_params(3, 48),
    )(tiles, slopes * LOG2E, q, k, v, k_tail, v_tail, *[vec(x) for x in lambdas],
      subln_g.reshape(-1, 1))


def _proj(h_in, w, n, dtype, ep=None, **kw):
    return _matmul([h_in], [(w, 0)], [(0, 0)], ep or _ep_plain, n, dtype, 1024, min(1024, n), 52, **kw)


def _prepare(meta_tokens, wts):
    bf = lambda w: w.astype(BF16)
    o1 = D_MODEL
    o2 = o1 + SSD_CONV_CH
    o3 = o2 + 2 * SSD_N_HEADS
    o4 = o3 + ATT_WIDTH
    o5 = o4 + ATT_WIDTH
    w_in = wts["w_in"]
    p = dict(wts)
    p["w_z"], p["w_xbc"], p["w_dt"] = bf(w_in[:, :o1]), bf(w_in[:, o1:o2]), bf(w_in[:, o2:o3])
    p["w_q"], p["w_k"], p["w_v"] = bf(w_in[:, o3:o4]), bf(w_in[:, o4:o5]), bf(w_in[:, o5:])
    for name in ("w_branch_gate", "w_ssd_out", "w_att_out", "w_o", "w_gate_up", "w_down"):
        p[name] = bf(wts[name])
    ones_bd = (jnp.arange(256)[:, None] // ATT_HALF_DIM
               == jnp.arange(256)[None, :] // ATT_HALF_DIM).astype(BF16)
    p["qk_norm"] = dict(
        q=dict(rows=[(jnp.tile(wts["q_norm_g"] * (ATT_HALF_DIM ** -0.5 * LOG2E),
                               2 * ATT_N_HEADS).reshape(1, ATT_WIDTH), 0)], consts=[ones_bd]),
        k=dict(rows=[(jnp.tile(wts["k_norm_g"], 2 * ATT_N_HEADS).reshape(1, ATT_WIDTH), 0)],
               consts=[ones_bd]))
    hn_meta = _rmsnorm(meta_tokens, wts["norm_mix_g"], N_META)
    tail_pad = ((0, LANES - N_META), (0, 0))
    p["xbc_meta"] = _proj(hn_meta, p["w_xbc"], SSD_CONV_CH, BF16)
    p["dt_raw_meta"] = _proj(hn_meta, p["w_dt"], 2 * SSD_N_HEADS, F32)
    p["k_tail"] = jnp.pad(_proj(hn_meta, p["w_k"], ATT_WIDTH, BF16, ep=_ep_qknorm,
                                **p["qk_norm"]["k"]), tail_pad)
    p["v_tail"] = jnp.pad(_proj(hn_meta, p["w_v"], ATT_WIDTH, BF16), tail_pad)
    return p


def _layer(x, layer_idx, p):
    b, s, d = x.shape
    m = b * s
    x2 = x.reshape(m, d)
    hn = _rmsnorm(x2, p["norm_mix_g"], 256)

    z = _proj(hn, p["w_z"], D_MODEL, BF16)
    xbc = _proj(hn, p["w_xbc"], SSD_CONV_CH, BF16)
    dt_raw = _proj(hn, p["w_dt"], 2 * SSD_N_HEADS, F32)
    q = _proj(hn, p["w_q"], ATT_WIDTH, BF16, ep=_ep_qknorm, **p["qk_norm"]["q"])
    k = _proj(hn, p["w_k"], ATT_WIDTH, BF16, ep=_ep_qknorm, **p["qk_norm"]["k"])
    v = _proj(hn, p["w_v"], ATT_WIDTH, BF16)

    xs, bmat, cmat, dt = _conv_dt(
        xbc.reshape(b, s, SSD_CONV_CH), p["xbc_meta"], p["conv_w"], p["conv_b"],
        dt_raw.reshape(b, s, 2 * SSD_N_HEADS), p["dt_raw_meta"], p["dt_bias"].reshape(-1))
    y_ssd = _ssd(xs, bmat, cmat, dt, p["a_log"].reshape(-1), p["d_skip"],
                 p["ssd_norm_g"], z.reshape(b, s, D_MODEL))

    y_att = _attention(
        q.reshape(b, s, ATT_WIDTH), k.reshape(b, s, ATT_WIDTH), v.reshape(b, s, ATT_WIDTH),
        p["k_tail"], p["v_tail"], p["q_norm_g"], p["k_norm_g"],
        (p["lambda_q1"], p["lambda_k1"], p["lambda_q2"], p["lambda_k2"]),
        p["subln_g"], layer_idx)

    gates = _matmul([hn], [(p["w_branch_gate"], 0)], [(0, 0)], _ep_sigmoid_bias,
                    2 * D_MODEL, BF16, 1024, 1024, 52,
                    rows=[(p["b_branch_gate"].reshape(1, -1), 0)])
    bn_mix = 512
    mixed = _matmul([y_ssd.reshape(m, D_MODEL), y_att.reshape(m, ATT_WIDTH)],
                    [(p["w_ssd_out"], 0), (p["w_att_out"], 0)],
                    [(0, 0), (1, 1)], _ep_gated_merge, D_MODEL, BF16, 1024, bn_mix, 56,
                    tiles=[(gates, 0), (gates, D_MODEL // bn_mix)])
    h2 = _matmul([mixed], [(p["w_o"], 0)], [(0, 0)], _ep_residual,
                 D_MODEL, F32, 1024, 512, 48, tiles=[(x2, 0)])

    hf = _rmsnorm(h2, p["norm_ffn_g"], 256)
    bn_ff = 256
    act = _matmul([hf], [(p["w_gate_up"], 0), (p["w_gate_up"], D_FF // bn_ff)],
                  [(0, 0), (0, 1)], _ep_swiglu, D_FF, BF16, 1024, bn_ff, 40)
    out = _matmul([act], [(p["w_down"], 0)], [(0, 0)], _ep_residual,
                  D_MODEL, F32, 512, 256, 48, tiles=[(h2, 0)])
    return out.reshape(b, s, d)


def kernel(x_prompt, x_sample, meta_tokens, norm_mix_g, w_in, conv_w, conv_b, dt_bias, a_log, d_skip, ssd_norm_g, q_norm_g, k_norm_g, lambda_q1, lambda_k1, lambda_q2, lambda_k2, subln_g, w_branch_gate, b_branch_gate, w_ssd_out, w_att_out, w_o, norm_ffn_g, w_gate_up, w_down):
    names = ("norm_mix_g", "w_in", "conv_w", "conv_b", "dt_bias", "a_log", "d_skip",
             "ssd_norm_g", "q_norm_g", "k_norm_g", "lambda_q1", "lambda_k1", "lambda_q2",
             "lambda_k2", "subln_g", "w_branch_gate", "b_branch_gate", "w_ssd_out",
             "w_att_out", "w_o", "norm_ffn_g", "w_gate_up", "w_down")
    stacked = (norm_mix_g, w_in, conv_w, conv_b, dt_bias, a_log, d_skip, ssd_norm_g,
               q_norm_g, k_norm_g, lambda_q1, lambda_k1, lambda_q2, lambda_k2, subln_g,
               w_branch_gate, b_branch_gate, w_ssd_out, w_att_out, w_o, norm_ffn_g,
               w_gate_up, w_down)
    assert all(w.shape[0] == 1 for w in stacked), "single-layer encoder"
    p = _prepare(meta_tokens, {n: w[0] for n, w in zip(names, stacked)})
    return (_layer(x_prompt, 0, p), _layer(x_sample, 0, p))
```

```python
import functools
import math

import jax
import jax.numpy as jnp
from jax import lax
from jax.experimental import pallas as pl
from jax.experimental.pallas import tpu as pltpu

F32 = jnp.float32
BF16 = jnp.bfloat16

D_MODEL = 4096
N_META = 16
EPS = 1e-6
SSD_HEAD_DIM = 64
SSD_N_HEADS = D_MODEL // SSD_HEAD_DIM
SSD_N_GROUPS = 8
SSD_HEADS_PER_GROUP = SSD_N_HEADS // SSD_N_GROUPS
SSD_D_STATE = 128
SSD_CONV = 5
SSD_CHUNK = 128
SSD_GN = SSD_N_GROUPS * SSD_D_STATE
SSD_CONV_CH = D_MODEL + 2 * SSD_GN
SSD_GROUP_W = SSD_HEADS_PER_GROUP * SSD_HEAD_DIM
ATT_N_HEADS = 16
ATT_HALF_DIM = 64
ATT_V_DIM = 128
ATT_WIDTH = ATT_N_HEADS * ATT_V_DIM
D_FF = 11008
LANES = 128
HALO = 16
MIB = 1024 * 1024
LOG2E = 1.4426950408889634


def _params(n_axes, vmem_mib):
    return pltpu.CompilerParams(
        dimension_semantics=("arbitrary",) * n_axes,
        vmem_limit_bytes=vmem_mib * MIB)


def _rmsnorm_kernel(x_ref, g_ref, o_ref):
    x = x_ref[...]
    ms = jnp.mean(x * x, axis=-1, keepdims=True)
    o_ref[...] = (x * lax.rsqrt(ms + EPS) * g_ref[...]).astype(o_ref.dtype)


def _rmsnorm(x, g, bm):
    m, d = x.shape
    bm = min(bm, m)
    return pl.pallas_call(
        _rmsnorm_kernel,
        grid=(m // bm,),
        in_specs=[pl.BlockSpec((bm, d), lambda i: (i, 0)),
                  pl.BlockSpec((1, d), lambda i: (0, 0))],
        out_specs=pl.BlockSpec((bm, d), lambda i: (i, 0)),
        out_shape=jax.ShapeDtypeStruct((m, d), BF16),
        compiler_params=_params(1, 32),
    )(x, g.reshape(1, d))


def _mm_kernel(*refs, dots, n_lhs, n_rhs, n_tiles, n_rows, n_consts, epilogue):
    lhs = refs[:n_lhs]
    rhs = refs[n_lhs:n_lhs + n_rhs]
    p = n_lhs + n_rhs
    tiles = refs[p:p + n_tiles]
    p += n_tiles
    rows = refs[p:p + n_rows]
    p += n_rows
    consts = refs[p:p + n_consts]
    o_ref = refs[-1]
    accs = [jnp.dot(lhs[a][...], rhs[b][...], preferred_element_type=F32)
            for a, b in dots]
    out = epilogue(accs, [t[...] for t in tiles], [r[...] for r in rows],
                   [c[...] for c in consts])
    o_ref[...] = out.astype(o_ref.dtype)


def _matmul(lhs, rhs, dots, epilogue, n_out, out_dtype, bm, bn, vmem_mib,
            tiles=(), rows=(), consts=()):
    m = lhs[0].shape[0]
    bm = min(bm, m)
    in_specs = []
    args = []
    for a in lhs:
        in_specs.append(pl.BlockSpec((bm, a.shape[1]), lambda i, j: (i, 0)))
        args.append(a)
    for w, off in rhs:
        in_specs.append(pl.BlockSpec((w.shape[0], bn), lambda i, j, off=off: (0, j + off)))
        args.append(w)
    for t, off in tiles:
        in_specs.append(pl.BlockSpec((bm, bn), lambda i, j, off=off: (i, j + off)))
        args.append(t)
    for r, off in rows:
        in_specs.append(pl.BlockSpec((1, bn), lambda i, j, off=off: (0, j + off)))
        args.append(r)
    for c in consts:
        in_specs.append(pl.BlockSpec(c.shape, lambda i, j, nd=c.ndim: (0,) * nd))
        args.append(c)
    kern = functools.partial(
        _mm_kernel, dots=tuple(dots), n_lhs=len(lhs), n_rhs=len(rhs),
        n_tiles=len(tiles), n_rows=len(rows), n_consts=len(consts), epilogue=epilogue)
    return pl.pallas_call(
        kern,
        grid=(m // bm, n_out // bn),
        in_specs=in_specs,
        out_specs=pl.BlockSpec((bm, bn), lambda i, j: (i, j)),
        out_shape=jax.ShapeDtypeStruct((m, n_out), out_dtype),
        compiler_params=_params(2, vmem_mib),
    )(*args)


def _ep_plain(accs, tiles, rows, consts):
    return accs[0]


def _ep_qknorm(accs, tiles, rows, consts):
    acc = accs[0]
    ones_bd = consts[0]
    width = ones_bd.shape[0]
    outs = []
    for c in range(acc.shape[1] // width):
        a = acc[:, c * width:(c + 1) * width]
        ss = jnp.dot((a * a).astype(BF16), ones_bd, preferred_element_type=F32)
        outs.append(a * lax.rsqrt(ss * (1.0 / ATT_HALF_DIM) + EPS))
    return jnp.concatenate(outs, axis=1) * rows[0]


def _ep_sigmoid_bias(accs, tiles, rows, consts):
    return 1.0 / (1.0 + jnp.exp(-(accs[0] + rows[0])))


def _ep_gated_merge(accs, tiles, rows, consts):
    return tiles[0].astype(F32) * accs[0] + tiles[1].astype(F32) * accs[1]


def _ep_residual(accs, tiles, rows, consts):
    return tiles[0] + accs[0]


def _ep_swiglu(accs, tiles, rows, consts):
    g = accs[0]
    return (g * (1.0 / (1.0 + jnp.exp(-g)))) * accs[1]


def _conv_kernel(cur_ref, prev_ref, next_ref, meta_ref, w_ref, b_ref, shift_ref,
                 dtr_ref, dtm_ref, dtb_ref,
                 xs_ref, bm_ref, cm_ref, dt_ref, ext_ref, *, n_chunks):
    c = pl.program_id(1)
    pad = SSD_CHUNK - N_META
    ch = ext_ref.shape[1]
    dtype = ext_ref.dtype

    @pl.when(c == 0)
    def _():
        ext_ref[0:HALO + pad, :] = jnp.zeros((HALO + pad, ch), dtype)
        ext_ref[HALO + pad:HALO + SSD_CHUNK, :] = meta_ref[...]

    @pl.when(c == 1)
    def _():
        ext_ref[0:HALO, :] = meta_ref[...]

    @pl.when(c > 1)
    def _():
        ext_ref[0:HALO, :] = prev_ref[0]

    @pl.when(c > 0)
    def _():
        ext_ref[HALO:HALO + SSD_CHUNK, :] = cur_ref[0]

    @pl.when(c < n_chunks - 1)
    def _():
        ext_ref[HALO + SSD_CHUNK:, :] = next_ref[0]

    @pl.when(c == n_chunks - 1)
    def _():
        ext_ref[HALO + SSD_CHUNK:, :] = jnp.zeros((HALO, ch), dtype)

    half = (SSD_CONV - 1) // 2

    def conv_silu(lo, hi):
        ext = ext_ref[:, lo:hi]
        shifted = jnp.dot(shift_ref[...], ext, preferred_element_type=F32)
        acc = None
        idx = 0
        for k in range(SSD_CONV):
            if k == half:
                x = ext[HALO:HALO + SSD_CHUNK].astype(F32)
            else:
                x = shifted[idx * SSD_CHUNK:(idx + 1) * SSD_CHUNK]
                idx += 1
            term = x * w_ref[k:k + 1, lo:hi]
            acc = term if acc is None else acc + term
        acc = acc + b_ref[:, lo:hi]
        return acc * (1.0 / (1.0 + jnp.exp(-acc)))

    blk = 1024
    for s in range(D_MODEL // blk):
        xs_ref[0, :, s * blk:(s + 1) * blk] = conv_silu(s * blk, (s + 1) * blk).astype(xs_ref.dtype)
    bm_ref[0] = conv_silu(D_MODEL, D_MODEL + SSD_GN).astype(bm_ref.dtype)
    cm_ref[0] = conv_silu(D_MODEL + SSD_GN, SSD_CONV_CH).astype(cm_ref.dtype)

    @pl.when(c == 0)
    def _():
        xs_ref[0, 0:pad, :] = jnp.zeros((pad, D_MODEL), xs_ref.dtype)
        bm_ref[0, 0:pad, :] = jnp.zeros((pad, SSD_GN), bm_ref.dtype)
        cm_ref[0, 0:pad, :] = jnp.zeros((pad, SSD_GN), cm_ref.dtype)

    row = lax.broadcasted_iota(jnp.int32, (SSD_CHUNK, dt_ref.shape[2]), 0)
    valid = jnp.logical_or(c > 0, row >= pad)

    raw_real = dtr_ref[0]
    raw_meta = jnp.concatenate(
        [jnp.zeros((pad, raw_real.shape[1]), F32), dtm_ref[...]], axis=0)
    raw = jnp.where(c > 0, raw_real, raw_meta) + dtb_ref[...]
    sp = jnp.maximum(raw, 0.0) + jnp.log1p(jnp.exp(-jnp.abs(raw)))
    dt_ref[0] = jnp.where(valid, sp, 0.0)


def _conv_dt(xbc, xbc_meta, conv_w, conv_b, dt_raw, dt_raw_meta, dt_bias):
    b, s, ch = xbc.shape
    n_chunks = s // SSD_CHUNK + 1
    lp = n_chunks * SSD_CHUNK
    hb = SSD_CHUNK // HALO
    n_halo = s // HALO
    nd = dt_raw.shape[-1]
    kern = functools.partial(_conv_kernel, n_chunks=n_chunks)
    n_ext = 2 * HALO + SSD_CHUNK
    half = (SSD_CONV - 1) // 2
    taps = jnp.array([k - half for k in range(SSD_CONV) if k != half], jnp.int32)
    src = HALO + jnp.arange(SSD_CHUNK, dtype=jnp.int32)[None, :] + taps[:, None]
    shift = (src.reshape(-1, 1) == jnp.arange(n_ext, dtype=jnp.int32)[None, :]).astype(BF16)
    return pl.pallas_call(
        kern,
        grid=(b, n_chunks),
        in_specs=[
            pl.BlockSpec((1, SSD_CHUNK, ch), lambda i, c: (i, jnp.maximum(c - 1, 0), 0)),
            pl.BlockSpec((1, HALO, ch), lambda i, c: (i, jnp.maximum((c - 1) * hb - 1, 0), 0)),
            pl.BlockSpec((1, HALO, ch), lambda i, c: (i, jnp.minimum(c * hb, n_halo - 1), 0)),
            pl.BlockSpec((N_META, ch), lambda i, c: (0, 0)),
            pl.BlockSpec((SSD_CONV, ch), lambda i, c: (0, 0)),
            pl.BlockSpec((1, ch), lambda i, c: (0, 0)),
            pl.BlockSpec(((SSD_CONV - 1) * SSD_CHUNK, n_ext), lambda i, c: (0, 0)),
            pl.BlockSpec((1, SSD_CHUNK, nd), lambda i, c: (i, jnp.maximum(c - 1, 0), 0)),
            pl.BlockSpec((N_META, nd), lambda i, c: (0, 0)),
            pl.BlockSpec((1, nd), lambda i, c: (0, 0)),
        ],
        out_specs=[
            pl.BlockSpec((1, SSD_CHUNK, D_MODEL), lambda i, c: (i, c, 0)),
            pl.BlockSpec((1, SSD_CHUNK, SSD_GN), lambda i, c: (i, c, 0)),
            pl.BlockSpec((1, SSD_CHUNK, SSD_GN), lambda i, c: (i, c, 0)),
            pl.BlockSpec((1, SSD_CHUNK, nd), lambda i, c: (i, c, 0)),
        ],
        out_shape=[
            jax.ShapeDtypeStruct((b, lp, D_MODEL), BF16),
            jax.ShapeDtypeStruct((b, lp, SSD_GN), BF16),
            jax.ShapeDtypeStruct((b, lp, SSD_GN), BF16),
            jax.ShapeDtypeStruct((b, lp, nd), F32),
        ],
        scratch_shapes=[pltpu.VMEM((n_ext, ch), BF16)],
        compiler_params=_params(2, 40),
    )(xbc, xbc, xbc, xbc_meta, conv_w, conv_b.reshape(1, ch), shift,
      dt_raw, dt_raw_meta, dt_bias.reshape(1, nd))


def _split3(x):
    hi = x.astype(BF16)
    r = x - hi.astype(F32)
    mid = r.astype(BF16)
    lo = (r - mid.astype(F32)).astype(BF16)
    return hi, mid, lo


def _ssd_body(x_ref, b_ref, c_ref, dt_ref, alog_ref, exp_ref, state_ref, *, reverse):
    t = SSD_CHUNK
    lane0 = SSD_N_HEADS if reverse else 0
    dt = dt_ref[0]
    a = dt * (-jnp.exp(alog_ref[...]) * LOG2E)
    ri = lax.broadcasted_iota(jnp.int32, (t, t), 0)
    ci = lax.broadcasted_iota(jnp.int32, (t, t), 1)
    causal = (ci >= ri) if reverse else (ci <= ri)
    tri = causal.astype(BF16)
    a3 = _split3(a)
    cs = sum(jnp.dot(tri, p, preferred_element_type=F32) for p in a3)
    cs_t = cs.T
    dt_t = dt.T
    edge = 0 if reverse else t - 1
    cs_edge = cs[edge:edge + 1, :]
    state_in = dt * jnp.exp2(cs_edge - cs)
    edge_parts = jnp.concatenate(
        [p.astype(F32) for p in _split3(jnp.exp2(cs_edge))]
        + [jnp.zeros((HALO - 3, LANES), F32)], axis=0)
    stacked = jnp.concatenate([state_in, edge_parts], axis=0).astype(BF16)
    expanded = jnp.dot(stacked, exp_ref[...], preferred_element_type=F32)
    chunk_decay_all = expanded[t:t + 1] + expanded[t + 1:t + 2] + expanded[t + 2:t + 3]
    first_half = lax.broadcasted_iota(jnp.int32, (t, LANES), 1) < SSD_HEAD_DIM
    outs = []
    for g in range(SSD_N_GROUPS):
        gs = slice(g * SSD_GROUP_W, (g + 1) * SSD_GROUP_W)
        ns = slice(g * SSD_D_STATE, (g + 1) * SSD_D_STATE)
        bg = b_ref[0, :, ns]
        cg = c_ref[0, :, ns]
        cb = lax.dot_general(cg, bg, (((1,), (1,)), ((), ())), preferred_element_type=F32)
        cbm = jnp.where(causal, cb, 0.0)
        xg = x_ref[0, :, gs]
        x_in = (xg.astype(F32) * expanded[0:t, gs]).astype(BF16)
        new_state = lax.dot_general(bg, x_in, (((0,), (0,)), ((), ())),
                                    preferred_element_type=F32)
        prev = state_ref[g]
        y_off = jnp.dot(cg, prev.astype(BF16), preferred_element_type=F32)
        state_ref[g] = prev * chunk_decay_all[:, gs] + new_state
        for jp in range(SSD_HEADS_PER_GROUP // 2):
            xp = xg[:, jp * LANES:(jp + 1) * LANES]
            zero = jnp.zeros_like(xp)
            rhs = jnp.concatenate([jnp.where(first_half, xp, zero),
                                   jnp.where(first_half, zero, xp)], axis=0)
            mats, ecols = [], []
            for e in range(2):
                hl = lane0 + g * SSD_HEADS_PER_GROUP + 2 * jp + e
                colb = jnp.broadcast_to(cs[:, hl:hl + 1], (t, t))
                seg = jnp.minimum(colb - cs_t[hl:hl + 1, :], 0.0)
                mats.append((cbm * jnp.exp2(seg) * dt_t[hl:hl + 1, :]).astype(BF16))
                ecols.append(jnp.exp2(colb))
            y_diag = jnp.dot(jnp.concatenate(mats, axis=1), rhs, preferred_element_type=F32)
            y_pair = y_diag + (y_off[:, jp * LANES:(jp + 1) * LANES]
                               * jnp.where(first_half, ecols[0], ecols[1]))
            outs.append(y_pair)
    return outs


def _ssd_rev_kernel(x_ref, b_ref, c_ref, dt_ref, alog_ref, exp_ref, y_ref, state_ref, *, n_chunks):
    step = pl.program_id(1)

    @pl.when(step == 0)
    def _():
        state_ref[...] = jnp.zeros(state_ref.shape, F32)

    outs = _ssd_body(x_ref, b_ref, c_ref, dt_ref, alog_ref, exp_ref, state_ref, reverse=True)

    @pl.when(step < n_chunks - 1)
    def _():
        for k, y in enumerate(outs):
            y_ref[0, :, k * LANES:(k + 1) * LANES] = y


def _ssd_fwd_kernel(x_ref, b_ref, c_ref, dt_ref, alog_ref, exp_ref, yb_ref, z_ref,
                    dskip_ref, gain_ref, o_ref, state_ref, ybuf_ref):
    step = pl.program_id(1)

    @pl.when(step == 0)
    def _():
        state_ref[...] = jnp.zeros(state_ref.shape, F32)

    outs = _ssd_body(x_ref, b_ref, c_ref, dt_ref, alog_ref, exp_ref, state_ref, reverse=False)

    @pl.when(step > 0)
    def _():
        for k, y in enumerate(outs):
            ls = slice(k * LANES, (k + 1) * LANES)
            x = x_ref[0, :, ls].astype(F32)
            z = z_ref[0, :, ls].astype(F32)
            yk = (y + yb_ref[0, :, ls]) + x * dskip_ref[:, ls]
            ybuf_ref[:, ls] = yk * (z * (1.0 / (1.0 + jnp.exp(-z))))
        for g in range(SSD_N_GROUPS):
            gs = slice(g * SSD_GROUP_W, (g + 1) * SSD_GROUP_W)
            yg = ybuf_ref[:, gs]
            ms = jnp.mean(yg * yg, axis=-1, keepdims=True)
            o_ref[0, :, gs] = (yg * lax.rsqrt(ms + EPS) * gain_ref[:, gs]).astype(o_ref.dtype)


def _ssd(xs, bmat, cmat, dt, a_log, d_skip, ssd_norm_g, z):
    b, lp, _ = xs.shape
    n_chunks = lp // SSD_CHUNK
    s = lp - SSD_CHUNK
    nd = dt.shape[-1]
    alog = a_log.reshape(1, nd)
    heads = jnp.arange(D_MODEL, dtype=jnp.int32) // SSD_HEAD_DIM
    lanes = jnp.arange(nd, dtype=jnp.int32)
    expand_f = (lanes[:, None] == heads[None, :]).astype(BF16)
    expand_r = (lanes[:, None] == heads[None, :] + SSD_N_HEADS).astype(BF16)
    dskip = jnp.repeat(d_skip, SSD_HEAD_DIM).reshape(1, D_MODEL)
    gain = ssd_norm_g.reshape(1, D_MODEL)
    state = pltpu.VMEM((SSD_N_GROUPS, SSD_D_STATE, SSD_GROUP_W), F32)

    def chunk_specs(cmap):
        return [
            pl.BlockSpec((1, SSD_CHUNK, D_MODEL), lambda i, c: (i, cmap(c), 0)),
            pl.BlockSpec((1, SSD_CHUNK, SSD_GN), lambda i, c: (i, cmap(c), 0)),
            pl.BlockSpec((1, SSD_CHUNK, SSD_GN), lambda i, c: (i, cmap(c), 0)),
            pl.BlockSpec((1, SSD_CHUNK, nd), lambda i, c: (i, cmap(c), 0)),
            pl.BlockSpec((1, nd), lambda i, c: (0, 0)),
            pl.BlockSpec((nd, D_MODEL), lambda i, c: (0, 0)),
        ]

    rmap = lambda c: n_chunks - 1 - c
    rout = lambda c: jnp.maximum(n_chunks - 2 - c, 0)
    y_rev = pl.pallas_call(
        functools.partial(_ssd_rev_kernel, n_chunks=n_chunks),
        grid=(b, n_chunks),
        in_specs=chunk_specs(rmap),
        out_specs=pl.BlockSpec((1, SSD_CHUNK, D_MODEL), lambda i, c: (i, rout(c), 0)),
        out_shape=jax.ShapeDtypeStruct((b, s, D_MODEL), F32),
        scratch_shapes=[state],
        compiler_params=_params(2, 48),
    )(xs, bmat, cmat, dt, alog, expand_r)

    fmap = lambda c: c
    fout = lambda c: jnp.maximum(c - 1, 0)
    return pl.pallas_call(
        _ssd_fwd_kernel,
        grid=(b, n_chunks),
        in_specs=chunk_specs(fmap) + [
            pl.BlockSpec((1, SSD_CHUNK, D_MODEL), lambda i, c: (i, fout(c), 0)),
            pl.BlockSpec((1, SSD_CHUNK, D_MODEL), lambda i, c: (i, fout(c), 0)),
            pl.BlockSpec((1, D_MODEL), lambda i, c: (0, 0)),
            pl.BlockSpec((1, D_MODEL), lambda i, c: (0, 0)),
        ],
        out_specs=pl.BlockSpec((1, SSD_CHUNK, D_MODEL), lambda i, c: (i, fout(c), 0)),
        out_shape=jax.ShapeDtypeStruct((b, s, D_MODEL), BF16),
        scratch_shapes=[state, pltpu.VMEM((SSD_CHUNK, D_MODEL), F32)],
        compiler_params=_params(2, 48),
    )(xs, bmat, cmat, dt, alog, expand_f, y_rev, z, dskip, gain)


ATT_ONES_ROWS = 16
ATT_DROP_NATS = 25.0


def _attn_kernel(tiles_ref, slopes_ref, q_ref, k_ref, v_ref, kt_ref, vt_ref,
                 lq1_ref, lk1_ref, lq2_ref, lk2_ref, sub_ref, o_ref,
                 vT_ref, vtT_ref, bias_ref, s0_ref, s1_ref, acc_ref, *, tq, tk, n_kt, n_qt, lam_init):
    h = pl.program_id(1)
    i = pl.program_id(2)
    lo = tiles_ref[2 * (h * n_qt + i)]
    hi = tiles_ref[2 * (h * n_qt + i) + 1]
    slope = slopes_ref[h]
    vrows = ATT_V_DIM + ATT_ONES_ROWS

    @pl.when(i == 0)
    def _():
        for t in range(n_kt):
            vT_ref[t, 0:ATT_V_DIM, :] = v_ref[0, t * tk:(t + 1) * tk, :].astype(F32).T.astype(BF16)
            vT_ref[t, ATT_V_DIM:vrows, :] = jnp.ones((ATT_ONES_ROWS, tk), BF16)
        vtT_ref[0:ATT_V_DIM, :] = vt_ref[...].astype(F32).T.astype(BF16)
        vtT_ref[ATT_V_DIM:vrows, :] = jnp.ones((ATT_ONES_ROWS, LANES), BF16)
        srel = slope * (lax.broadcasted_iota(jnp.int32, (tk, tq), 0)
                        - lax.broadcasted_iota(jnp.int32, (tk, tq), 1)).astype(F32)
        bias_ref[0] = -srel
        bias_ref[1] = srel
        bias_ref[2] = jnp.abs(srel)

    q = q_ref[0]
    lane = lax.broadcasted_iota(jnp.int32, q.shape, 1)
    zero = jnp.zeros_like(q)
    qs = (jnp.where(lane < ATT_HALF_DIM, q, zero), jnp.where(lane >= ATT_HALF_DIM, q, zero))

    def scores(k, bias_tile):
        return [lax.dot_general(k, qs[e], (((1,), (1,)), ((), ())),
                                preferred_element_type=F32) - bias_tile for e in range(2)]

    def probs(s, m):
        return jnp.exp2(s - m).astype(BF16)

    q0 = (i * tq).astype(F32)
    n_tail = kt_ref.shape[0]
    tail_valid = lax.broadcasted_iota(jnp.int32, (n_tail, 1), 0) < N_META
    c_tail = slope * (q0 + N_META)
    ms = []
    for e, s in enumerate(scores(kt_ref[...], bias_ref[0, 0:n_tail, :])):
        s = jnp.where(tail_valid, s, -jnp.inf)
        m = jnp.max(s, axis=0, keepdims=True)
        acc_ref[e] = jnp.dot(vtT_ref[...], probs(s, m), preferred_element_type=F32)
        ms.append(m - c_tail)

    def stage_a(t, dst_ref):
        start = pl.multiple_of(t * tk, tk)
        side = jnp.where(t < i, 0, jnp.where(t > i, 1, 2))
        c = slope * jnp.abs(start.astype(F32) - q0)
        ss = scores(k_ref[0, pl.ds(start, tk), :], bias_ref[side])
        for e in range(2):
            dst_ref[e] = ss[e]
        return tuple(jnp.max(ss[e], axis=0, keepdims=True) - c for e in range(2)) + (c,)

    def stage_b(t, src_ref, ms, cmax):
        vt_aug = vT_ref[t]
        out = []
        for e in range(2):
            m_new = jnp.maximum(ms[e], cmax[e])
            alpha = jnp.exp2(ms[e] - m_new)
            p = probs(src_ref[e], m_new + cmax[2])
            acc_ref[e] = alpha * acc_ref[e] + jnp.dot(vt_aug, p, preferred_element_type=F32)
            out.append(m_new)
        return tuple(out)

    def two_trips(u, carry):
        ms, cmax0 = carry
        t = lo + 2 * u
        cmax1 = stage_a(t + 1, s1_ref)
        ms = stage_b(t, s0_ref, ms, cmax0)
        cmax0 = stage_a(t + 2, s0_ref)
        ms = stage_b(t + 1, s1_ref, ms, cmax1)
        return ms, cmax0

    n_tiles = hi - lo
    n_two = (n_tiles - 1) // 2
    ms, cmax0 = lax.fori_loop(0, n_two, two_trips, (tuple(ms), stage_a(lo, s0_ref)))
    t_last = lo + 2 * n_two

    def finish_two(ms, cmax0):
        cmax1 = stage_a(t_last + 1, s1_ref)
        stage_b(t_last + 1, s1_ref, stage_b(t_last, s0_ref, ms, cmax0), cmax1)

    def finish_one(ms, cmax0):
        stage_b(t_last, s0_ref, ms, cmax0)

    lax.cond(n_tiles - 2 * n_two == 2, finish_two, finish_one, ms, cmax0)
    accs = (acc_ref[0], acc_ref[1])

    lam = (jnp.exp(jnp.sum(lq1_ref[...] * lk1_ref[...], axis=-1, keepdims=True))
           - jnp.exp(jnp.sum(lq2_ref[...] * lk2_ref[...], axis=-1, keepdims=True))
           + lam_init)
    o1, o2 = (a[0:ATT_V_DIM] / a[ATT_V_DIM:ATT_V_DIM + 1] for a in accs)
    o = o1 - lam * o2
    msq = jnp.mean(o * o, axis=0, keepdims=True)
    o = (o * lax.rsqrt(msq + EPS) * sub_ref[...]) * (1.0 - lam_init)
    o_ref[0] = o.T.astype(o_ref.dtype)


def _attention(q, k, v, k_tail, v_tail, q_gain, k_gain, lambdas, subln_g, layer_idx):
    b, s, _ = q.shape
    tq = tk = min(512, s)
    n_qt, n_kt = s // tq, s // tk
    lam_init = 0.8 - 0.6 * math.exp(-0.3 * layer_idx)
    slopes = jnp.exp2(-8.0 * jnp.arange(1, ATT_N_HEADS + 1, dtype=F32) / ATT_N_HEADS)
    bound = 1.01 * ATT_HALF_DIM ** 0.5 * jnp.max(jnp.abs(q_gain)) * jnp.max(jnp.abs(k_gain))
    reach = jnp.ceil((2.0 * bound + ATT_DROP_NATS) / slopes)
    reach = jnp.minimum(reach, float(2 * s)).astype(jnp.int32)[:, None]
    q_first = (jnp.arange(n_qt, dtype=jnp.int32) * tq)[None, :]
    lo = jnp.maximum(q_first - reach, 0) // tk
    hi = jnp.minimum((q_first + tq - 1 + reach) // tk + 1, n_kt)
    tiles = jnp.stack([lo, hi], axis=-1).reshape(-1).astype(jnp.int32)
    kern = functools.partial(_attn_kernel, tq=tq, tk=tk, n_kt=n_kt, n_qt=n_qt, lam_init=lam_init)
    vec = lambda x: x.reshape(1, -1)
    small = lambda n: pl.BlockSpec((1, n), lambda bi, h, i: (0, 0))
    vrows = ATT_V_DIM + ATT_ONES_ROWS
    return pl.pallas_call(
        kern,
        grid=(b, ATT_N_HEADS, n_qt),
        in_specs=[
            pl.BlockSpec(memory_space=pltpu.SMEM),
            pl.BlockSpec(memory_space=pltpu.SMEM),
            pl.BlockSpec((1, tq, LANES), lambda bi, h, i: (bi, i, h)),
            pl.BlockSpec((1, s, LANES), lambda bi, h, i: (bi, 0, h)),
            pl.BlockSpec((1, s, LANES), lambda bi, h, i: (bi, 0, h)),
            pl.BlockSpec((LANES, LANES), lambda bi, h, i: (0, h)),
            pl.BlockSpec((LANES, LANES), lambda bi, h, i: (0, h)),
            small(ATT_HALF_DIM), small(ATT_HALF_DIM), small(ATT_HALF_DIM), small(ATT_HALF_DIM),
            pl.BlockSpec((ATT_V_DIM, 1), lambda bi, h, i: (0, 0)),
        ],
        out_specs=pl.BlockSpec((1, tq, LANES), lambda bi, h, i: (bi, i, h)),
        out_shape=jax.ShapeDtypeStruct((b, s, ATT_WIDTH), BF16),
        scratch_shapes=[
            pltpu.VMEM((n_kt, vrows, tk), BF16),
            pltpu.VMEM((vrows, LANES), BF16),
            pltpu.VMEM((3, tk, tq), F32),
            pltpu.VMEM((2, tk, tq), F32),
            pltpu.VMEM((2, tk, tq), F32),
            pltpu.VMEM((2, vrows, tq), F32),
        ],
        compiler_params=_params(3, 48),
    )(tiles, slopes * LOG2E, q, k, v, k_tail, v_tail, *[vec(x) for x in lambdas],
      subln_g.reshape(-1, 1))


def _proj(h_in, w, n, dtype, ep=None, **kw):
    return _matmul([h_in], [(w, 0)], [(0, 0)], ep or _ep_plain, n, dtype, 1024, min(1024, n), 52, **kw)


def _prepare(meta_tokens, wts):
    bf = lambda w: w.astype(BF16)
    o1 = D_MODEL
    o2 = o1 + SSD_CONV_CH
    o3 = o2 + 2 * SSD_N_HEADS
    o4 = o3 + ATT_WIDTH
    o5 = o4 + ATT_WIDTH
    w_in = wts["w_in"]
    p = dict(wts)
    p["w_z"], p["w_xbc"], p["w_dt"] = bf(w_in[:, :o1]), bf(w_in[:, o1:o2]), bf(w_in[:, o2:o3])
    p["w_q"], p["w_k"], p["w_v"] = bf(w_in[:, o3:o4]), bf(w_in[:, o4:o5]), bf(w_in[:, o5:])
    for name in ("w_branch_gate", "w_ssd_out", "w_att_out", "w_o", "w_gate_up", "w_down"):
        p[name] = bf(wts[name])
    ones_bd = (jnp.arange(256)[:, None] // ATT_HALF_DIM
               == jnp.arange(256)[None, :] // ATT_HALF_DIM).astype(BF16)
    p["qk_norm"] = dict(
        q=dict(rows=[(jnp.tile(wts["q_norm_g"] * (ATT_HALF_DIM ** -0.5 * LOG2E),
                               2 * ATT_N_HEADS).reshape(1, ATT_WIDTH), 0)], consts=[ones_bd]),
        k=dict(rows=[(jnp.tile(wts["k_norm_g"], 2 * ATT_N_HEADS).reshape(1, ATT_WIDTH), 0)],
               consts=[ones_bd]))
    hn_meta = _rmsnorm(meta_tokens, wts["norm_mix_g"], N_META)
    tail_pad = ((0, LANES - N_META), (0, 0))
    p["xbc_meta"] = _proj(hn_meta, p["w_xbc"], SSD_CONV_CH, BF16)
    p["dt_raw_meta"] = _proj(hn_meta, p["w_dt"], 2 * SSD_N_HEADS, F32)
    p["k_tail"] = jnp.pad(_proj(hn_meta, p["w_k"], ATT_WIDTH, BF16, ep=_ep_qknorm,
                                **p["qk_norm"]["k"]), tail_pad)
    p["v_tail"] = jnp.pad(_proj(hn_meta, p["w_v"], ATT_WIDTH, BF16), tail_pad)
    return p


def _layer(x, layer_idx, p):
    b, s, d = x.shape
    m = b * s
    x2 = x.reshape(m, d)
    hn = _rmsnorm(x2, p["norm_mix_g"], 256)

    z = _proj(hn, p["w_z"], D_MODEL, BF16)
    xbc = _proj(hn, p["w_xbc"], SSD_CONV_CH, BF16)
    dt_raw = _proj(hn, p["w_dt"], 2 * SSD_N_HEADS, F32)
    q = _proj(hn, p["w_q"], ATT_WIDTH, BF16, ep=_ep_qknorm, **p["qk_norm"]["q"])
    k = _proj(hn, p["w_k"], ATT_WIDTH, BF16, ep=_ep_qknorm, **p["qk_norm"]["k"])
    v = _proj(hn, p["w_v"], ATT_WIDTH, BF16)

    xs, bmat, cmat, dt = _conv_dt(
        xbc.reshape(b, s, SSD_CONV_CH), p["xbc_meta"], p["conv_w"], p["conv_b"],
        dt_raw.reshape(b, s, 2 * SSD_N_HEADS), p["dt_raw_meta"], p["dt_bias"].reshape(-1))
    y_ssd = _ssd(xs, bmat, cmat, dt, p["a_log"].reshape(-1), p["d_skip"],
                 p["ssd_norm_g"], z.reshape(b, s, D_MODEL))

    y_att = _attention(
        q.reshape(b, s, ATT_WIDTH), k.reshape(b, s, ATT_WIDTH), v.reshape(b, s, ATT_WIDTH),
        p["k_tail"], p["v_tail"], p["q_norm_g"], p["k_norm_g"],
        (p["lambda_q1"], p["lambda_k1"], p["lambda_q2"], p["lambda_k2"]),
        p["subln_g"], layer_idx)

    gates = _matmul([hn], [(p["w_branch_gate"], 0)], [(0, 0)], _ep_sigmoid_bias,
                    2 * D_MODEL, BF16, 1024, 1024, 52,
                    rows=[(p["b_branch_gate"].reshape(1, -1), 0)])
    bn_mix = 512
    mixed = _matmul([y_ssd.reshape(m, D_MODEL), y_att.reshape(m, ATT_WIDTH)],
                    [(p["w_ssd_out"], 0), (p["w_att_out"], 0)],
                    [(0, 0), (1, 1)], _ep_gated_merge, D_MODEL, BF16, 1024, bn_mix, 56,
                    tiles=[(gates, 0), (gates, D_MODEL // bn_mix)])
    h2 = _matmul([mixed], [(p["w_o"], 0)], [(0, 0)], _ep_residual,
                 D_MODEL, F32, 1024, 512, 48, tiles=[(x2, 0)])

    hf = _rmsnorm(h2, p["norm_ffn_g"], 256)
    bn_ff = 256
    act = _matmul([hf], [(p["w_gate_up"], 0), (p["w_gate_up"], D_FF // bn_ff)],
                  [(0, 0), (0, 1)], _ep_swiglu, D_FF, BF16, 1024, bn_ff, 40)
    out = _matmul([act], [(p["w_down"], 0)], [(0, 0)], _ep_residual,
                  D_MODEL, F32, 512, 512, 56, tiles=[(h2, 0)])
    return out.reshape(b, s, d)


def kernel(x_prompt, x_sample, meta_tokens, norm_mix_g, w_in, conv_w, conv_b, dt_bias, a_log, d_skip, ssd_norm_g, q_norm_g, k_norm_g, lambda_q1, lambda_k1, lambda_q2, lambda_k2, subln_g, w_branch_gate, b_branch_gate, w_ssd_out, w_att_out, w_o, norm_ffn_g, w_gate_up, w_down):
    names = ("norm_mix_g", "w_in", "conv_w", "conv_b", "dt_bias", "a_log", "d_skip",
             "ssd_norm_g", "q_norm_g", "k_norm_g", "lambda_q1", "lambda_k1", "lambda_q2",
             "lambda_k2", "subln_g", "w_branch_gate", "b_branch_gate", "w_ssd_out",
             "w_att_out", "w_o", "norm_ffn_g", "w_gate_up", "w_down")
    stacked = (norm_mix_g, w_in, conv_w, conv_b, dt_bias, a_log, d_skip, ssd_norm_g,
               q_norm_g, k_norm_g, lambda_q1, lambda_k1, lambda_q2, lambda_k2, subln_g,
               w_branch_gate, b_branch_gate, w_ssd_out, w_att_out, w_o, norm_ffn_g,
               w_gate_up, w_down)
    assert all(w.shape[0] == 1 for w in stacked), "single-layer encoder"
    p = _prepare(meta_tokens, {n: w[0] for n, w in zip(names, stacked)})
    return (_layer(x_prompt, 0, p), _layer(x_sample, 0, p))
```

```python
import functools
import math

import jax
import jax.numpy as jnp
from jax import lax
from jax.experimental import pallas as pl
from jax.experimental.pallas import tpu as pltpu

F32 = jnp.float32
BF16 = jnp.bfloat16

D_MODEL = 4096
N_META = 16
EPS = 1e-6
SSD_HEAD_DIM = 64
SSD_N_HEADS = D_MODEL // SSD_HEAD_DIM
SSD_N_GROUPS = 8
SSD_HEADS_PER_GROUP = SSD_N_HEADS // SSD_N_GROUPS
SSD_D_STATE = 128
SSD_CONV = 5
SSD_CHUNK = 128
SSD_GN = SSD_N_GROUPS * SSD_D_STATE
SSD_CONV_CH = D_MODEL + 2 * SSD_GN
SSD_GROUP_W = SSD_HEADS_PER_GROUP * SSD_HEAD_DIM
ATT_N_HEADS = 16
ATT_HALF_DIM = 64
ATT_V_DIM = 128
ATT_WIDTH = ATT_N_HEADS * ATT_V_DIM
D_FF = 11008
LANES = 128
HALO = 16
MIB = 1024 * 1024
LOG2E = 1.4426950408889634


def _params(n_axes, vmem_mib):
    return pltpu.CompilerParams(
        dimension_semantics=("arbitrary",) * n_axes,
        vmem_limit_bytes=vmem_mib * MIB)


def _rmsnorm_kernel(x_ref, g_ref, o_ref):
    x = x_ref[...]
    ms = jnp.mean(x * x, axis=-1, keepdims=True)
    o_ref[...] = (x * lax.rsqrt(ms + EPS) * g_ref[...]).astype(o_ref.dtype)


def _rmsnorm(x, g, bm):
    m, d = x.shape
    bm = min(bm, m)
    return pl.pallas_call(
        _rmsnorm_kernel,
        grid=(m // bm,),
        in_specs=[pl.BlockSpec((bm, d), lambda i: (i, 0)),
                  pl.BlockSpec((1, d), lambda i: (0, 0))],
        out_specs=pl.BlockSpec((bm, d), lambda i: (i, 0)),
        out_shape=jax.ShapeDtypeStruct((m, d), BF16),
        compiler_params=_params(1, 32),
    )(x, g.reshape(1, d))


def _mm_kernel(*refs, dots, n_lhs, n_rhs, n_tiles, n_rows, n_consts, epilogue, emit_stats):
    lhs = refs[:n_lhs]
    rhs = refs[n_lhs:n_lhs + n_rhs]
    p = n_lhs + n_rhs
    tiles = refs[p:p + n_tiles]
    p += n_tiles
    rows = refs[p:p + n_rows]
    p += n_rows
    consts = refs[p:p + n_consts]
    p += n_consts
    o_ref = refs[p]
    accs = [jnp.dot(lhs[a][...], rhs[b][...], preferred_element_type=F32)
            for a, b in dots]
    out = epilogue(accs, [t[...] for t in tiles], [r[...] for r in rows],
                   [c[...] for c in consts])
    o_ref[...] = out.astype(o_ref.dtype)
    if emit_stats:
        obf_ref, ss_ref = refs[p + 1], refs[p + 2]
        obf_ref[...] = out.astype(obf_ref.dtype)

        @pl.when(pl.program_id(1) == 0)
        def _():
            ss_ref[...] = jnp.zeros(ss_ref.shape, F32)

        ss_ref[...] += jnp.broadcast_to(jnp.sum(out * out, axis=1, keepdims=True), ss_ref.shape)


def _matmul(lhs, rhs, dots, epilogue, n_out, out_dtype, bm, bn, vmem_mib,
            tiles=(), rows=(), consts=(), row_blocks=(), emit_stats=False):
    m = lhs[0].shape[0]
    bm = min(bm, m)
    in_specs = []
    args = []
    for a in lhs:
        in_specs.append(pl.BlockSpec((bm, a.shape[1]), lambda i, j: (i, 0)))
        args.append(a)
    for w, off in rhs:
        in_specs.append(pl.BlockSpec((w.shape[0], bn), lambda i, j, off=off: (0, j + off)))
        args.append(w)
    for t, off in tiles:
        in_specs.append(pl.BlockSpec((bm, bn), lambda i, j, off=off: (i, j + off)))
        args.append(t)
    for t in row_blocks:
        in_specs.append(pl.BlockSpec((bm, t.shape[1]), lambda i, j: (i, 0)))
        args.append(t)
    for r, off in rows:
        in_specs.append(pl.BlockSpec((1, bn), lambda i, j, off=off: (0, j + off)))
        args.append(r)
    for c in consts:
        in_specs.append(pl.BlockSpec(c.shape, lambda i, j, nd=c.ndim: (0,) * nd))
        args.append(c)
    out_specs = pl.BlockSpec((bm, bn), lambda i, j: (i, j))
    out_shape = jax.ShapeDtypeStruct((m, n_out), out_dtype)
    if emit_stats:
        out_specs = [out_specs, pl.BlockSpec((bm, bn), lambda i, j: (i, j)),
                     pl.BlockSpec((bm, LANES), lambda i, j: (i, 0))]
        out_shape = [out_shape, jax.ShapeDtypeStruct((m, n_out), BF16),
                     jax.ShapeDtypeStruct((m, LANES), F32)]
    kern = functools.partial(
        _mm_kernel, dots=tuple(dots), n_lhs=len(lhs), n_rhs=len(rhs),
        n_tiles=len(tiles) + len(row_blocks), n_rows=len(rows), n_consts=len(consts),
        epilogue=epilogue, emit_stats=emit_stats)
    return pl.pallas_call(
        kern,
        grid=(m // bm, n_out // bn),
        in_specs=in_specs,
        out_specs=out_specs,
        out_shape=out_shape,
        compiler_params=_params(2, vmem_mib),
    )(*args)


def _ep_plain(accs, tiles, rows, consts):
    return accs[0]


def _ep_qknorm(accs, tiles, rows, consts):
    acc = accs[0]
    ones_bd = consts[0]
    width = ones_bd.shape[0]
    outs = []
    for c in range(acc.shape[1] // width):
        a = acc[:, c * width:(c + 1) * width]
        ss = jnp.dot((a * a).astype(BF16), ones_bd, preferred_element_type=F32)
        outs.append(a * lax.rsqrt(ss * (1.0 / ATT_HALF_DIM) + EPS))
    return jnp.concatenate(outs, axis=1) * rows[0]


def _ep_sigmoid_bias(accs, tiles, rows, consts):
    return 1.0 / (1.0 + jnp.exp(-(accs[0] + rows[0])))


def _ep_gated_merge(accs, tiles, rows, consts):
    return tiles[0].astype(F32) * accs[0] + tiles[1].astype(F32) * accs[1]


def _ep_residual(accs, tiles, rows, consts):
    return tiles[0] + accs[0]


def _ep_swiglu(accs, tiles, rows, consts):
    inv_rms = lax.rsqrt(tiles[0][:, 0:1] * (1.0 / D_MODEL) + EPS)
    g = accs[0] * inv_rms
    return (g * (1.0 / (1.0 + jnp.exp(-g)))) * (accs[1] * inv_rms)


def _conv_kernel(cur_ref, prev_ref, next_ref, meta_ref, w_ref, b_ref, shift_ref,
                 dtr_ref, dtm_ref, dtb_ref,
                 xs_ref, bm_ref, cm_ref, dt_ref, ext_ref, *, n_chunks):
    c = pl.program_id(1)
    pad = SSD_CHUNK - N_META
    ch = ext_ref.shape[1]
    dtype = ext_ref.dtype

    @pl.when(c == 0)
    def _():
        ext_ref[0:HALO + pad, :] = jnp.zeros((HALO + pad, ch), dtype)
        ext_ref[HALO + pad:HALO + SSD_CHUNK, :] = meta_ref[...]

    @pl.when(c == 1)
    def _():
        ext_ref[0:HALO, :] = meta_ref[...]

    @pl.when(c > 1)
    def _():
        ext_ref[0:HALO, :] = prev_ref[0]

    @pl.when(c > 0)
    def _():
        ext_ref[HALO:HALO + SSD_CHUNK, :] = cur_ref[0]

    @pl.when(c < n_chunks - 1)
    def _():
        ext_ref[HALO + SSD_CHUNK:, :] = next_ref[0]

    @pl.when(c == n_chunks - 1)
    def _():
        ext_ref[HALO + SSD_CHUNK:, :] = jnp.zeros((HALO, ch), dtype)

    half = (SSD_CONV - 1) // 2

    def conv_silu(lo, hi):
        ext = ext_ref[:, lo:hi]
        shifted = jnp.dot(shift_ref[...], ext, preferred_element_type=F32)
        acc = None
        idx = 0
        for k in range(SSD_CONV):
            if k == half:
                x = ext[HALO:HALO + SSD_CHUNK].astype(F32)
            else:
                x = shifted[idx * SSD_CHUNK:(idx + 1) * SSD_CHUNK]
                idx += 1
            term = x * w_ref[k:k + 1, lo:hi]
            acc = term if acc is None else acc + term
        acc = acc + b_ref[:, lo:hi]
        return acc * (1.0 / (1.0 + jnp.exp(-acc)))

    blk = 1024
    for s in range(D_MODEL // blk):
        xs_ref[0, :, s * blk:(s + 1) * blk] = conv_silu(s * blk, (s + 1) * blk).astype(xs_ref.dtype)
    bm_ref[0] = conv_silu(D_MODEL, D_MODEL + SSD_GN).astype(bm_ref.dtype)
    cm_ref[0] = conv_silu(D_MODEL + SSD_GN, SSD_CONV_CH).astype(cm_ref.dtype)

    @pl.when(c == 0)
    def _():
        xs_ref[0, 0:pad, :] = jnp.zeros((pad, D_MODEL), xs_ref.dtype)
        bm_ref[0, 0:pad, :] = jnp.zeros((pad, SSD_GN), bm_ref.dtype)
        cm_ref[0, 0:pad, :] = jnp.zeros((pad, SSD_GN), cm_ref.dtype)

    row = lax.broadcasted_iota(jnp.int32, (SSD_CHUNK, dt_ref.shape[2]), 0)
    valid = jnp.logical_or(c > 0, row >= pad)

    raw_real = dtr_ref[0]
    raw_meta = jnp.concatenate(
        [jnp.zeros((pad, raw_real.shape[1]), F32), dtm_ref[...]], axis=0)
    raw = jnp.where(c > 0, raw_real, raw_meta) + dtb_ref[...]
    sp = jnp.maximum(raw, 0.0) + jnp.log1p(jnp.exp(-jnp.abs(raw)))
    dt_ref[0] = jnp.where(valid, sp, 0.0)


def _conv_dt(xbc, xbc_meta, conv_w, conv_b, dt_raw, dt_raw_meta, dt_bias):
    b, s, ch = xbc.shape
    n_chunks = s // SSD_CHUNK + 1
    lp = n_chunks * SSD_CHUNK
    hb = SSD_CHUNK // HALO
    n_halo = s // HALO
    nd = dt_raw.shape[-1]
    kern = functools.partial(_conv_kernel, n_chunks=n_chunks)
    n_ext = 2 * HALO + SSD_CHUNK
    half = (SSD_CONV - 1) // 2
    taps = jnp.array([k - half for k in range(SSD_CONV) if k != half], jnp.int32)
    src = HALO + jnp.arange(SSD_CHUNK, dtype=jnp.int32)[None, :] + taps[:, None]
    shift = (src.reshape(-1, 1) == jnp.arange(n_ext, dtype=jnp.int32)[None, :]).astype(BF16)
    return pl.pallas_call(
        kern,
        grid=(b, n_chunks),
        in_specs=[
            pl.BlockSpec((1, SSD_CHUNK, ch), lambda i, c: (i, jnp.maximum(c - 1, 0), 0)),
            pl.BlockSpec((1, HALO, ch), lambda i, c: (i, jnp.maximum((c - 1) * hb - 1, 0), 0)),
            pl.BlockSpec((1, HALO, ch), lambda i, c: (i, jnp.minimum(c * hb, n_halo - 1), 0)),
            pl.BlockSpec((N_META, ch), lambda i, c: (0, 0)),
            pl.BlockSpec((SSD_CONV, ch), lambda i, c: (0, 0)),
            pl.BlockSpec((1, ch), lambda i, c: (0, 0)),
            pl.BlockSpec(((SSD_CONV - 1) * SSD_CHUNK, n_ext), lambda i, c: (0, 0)),
            pl.BlockSpec((1, SSD_CHUNK, nd), lambda i, c: (i, jnp.maximum(c - 1, 0), 0)),
            pl.BlockSpec((N_META, nd), lambda i, c: (0, 0)),
            pl.BlockSpec((1, nd), lambda i, c: (0, 0)),
        ],
        out_specs=[
            pl.BlockSpec((1, SSD_CHUNK, D_MODEL), lambda i, c: (i, c, 0)),
            pl.BlockSpec((1, SSD_CHUNK, SSD_GN), lambda i, c: (i, c, 0)),
            pl.BlockSpec((1, SSD_CHUNK, SSD_GN), lambda i, c: (i, c, 0)),
            pl.BlockSpec((1, SSD_CHUNK, nd), lambda i, c: (i, c, 0)),
        ],
        out_shape=[
            jax.ShapeDtypeStruct((b, lp, D_MODEL), BF16),
            jax.ShapeDtypeStruct((b, lp, SSD_GN), BF16),
            jax.ShapeDtypeStruct((b, lp, SSD_GN), BF16),
            jax.ShapeDtypeStruct((b, lp, nd), F32),
        ],
        scratch_shapes=[pltpu.VMEM((n_ext, ch), BF16)],
        compiler_params=_params(2, 40),
    )(xbc, xbc, xbc, xbc_meta, conv_w, conv_b.reshape(1, ch), shift,
      dt_raw, dt_raw_meta, dt_bias.reshape(1, nd))


def _split3(x):
    hi = x.astype(BF16)
    r = x - hi.astype(F32)
    mid = r.astype(BF16)
    lo = (r - mid.astype(F32)).astype(BF16)
    return hi, mid, lo


def _ssd_body(x_ref, b_ref, c_ref, dt_ref, alog_ref, exp_ref, state_ref, *, reverse):
    t = SSD_CHUNK
    lane0 = SSD_N_HEADS if reverse else 0
    dt = dt_ref[0]
    a = dt * (-jnp.exp(alog_ref[...]) * LOG2E)
    ri = lax.broadcasted_iota(jnp.int32, (t, t), 0)
    ci = lax.broadcasted_iota(jnp.int32, (t, t), 1)
    causal = (ci >= ri) if reverse else (ci <= ri)
    tri = causal.astype(BF16)
    a3 = _split3(a)
    cs = sum(jnp.dot(tri, p, preferred_element_type=F32) for p in a3)
    cs_t = cs.T
    dt_t = dt.T
    edge = 0 if reverse else t - 1
    cs_edge = cs[edge:edge + 1, :]
    state_in = dt * jnp.exp2(cs_edge - cs)
    edge_parts = jnp.concatenate(
        [p.astype(F32) for p in _split3(jnp.exp2(cs_edge))]
        + [jnp.zeros((HALO - 3, LANES), F32)], axis=0)
    stacked = jnp.concatenate([state_in, edge_parts], axis=0).astype(BF16)
    expanded = jnp.dot(stacked, exp_ref[...], preferred_element_type=F32)
    chunk_decay_all = expanded[t:t + 1] + expanded[t + 1:t + 2] + expanded[t + 2:t + 3]
    first_half = lax.broadcasted_iota(jnp.int32, (t, LANES), 1) < SSD_HEAD_DIM
    outs = []
    for g in range(SSD_N_GROUPS):
        gs = slice(g * SSD_GROUP_W, (g + 1) * SSD_GROUP_W)
        ns = slice(g * SSD_D_STATE, (g + 1) * SSD_D_STATE)
        bg = b_ref[0, :, ns]
        cg = c_ref[0, :, ns]
        cb = lax.dot_general(cg, bg, (((1,), (1,)), ((), ())), preferred_element_type=F32)
        cbm = jnp.where(causal, cb, 0.0)
        xg = x_ref[0, :, gs]
        x_in = (xg.astype(F32) * expanded[0:t, gs]).astype(BF16)
        new_state = lax.dot_general(bg, x_in, (((0,), (0,)), ((), ())),
                                    preferred_element_type=F32)
        prev = state_ref[g]
        y_off = jnp.dot(cg, prev.astype(BF16), preferred_element_type=F32)
        state_ref[g] = prev * chunk_decay_all[:, gs] + new_state
        for jp in range(SSD_HEADS_PER_GROUP // 2):
            xp = xg[:, jp * LANES:(jp + 1) * LANES]
            zero = jnp.zeros_like(xp)
            rhs = jnp.concatenate([jnp.where(first_half, xp, zero),
                                   jnp.where(first_half, zero, xp)], axis=0)
            mats, ecols = [], []
            for e in range(2):
                hl = lane0 + g * SSD_HEADS_PER_GROUP + 2 * jp + e
                colb = jnp.broadcast_to(cs[:, hl:hl + 1], (t, t))
                seg = jnp.minimum(colb - cs_t[hl:hl + 1, :], 0.0)
                mats.append((cbm * jnp.exp2(seg) * dt_t[hl:hl + 1, :]).astype(BF16))
                ecols.append(jnp.exp2(colb))
            y_diag = jnp.dot(jnp.concatenate(mats, axis=1), rhs, preferred_element_type=F32)
            y_pair = y_diag + (y_off[:, jp * LANES:(jp + 1) * LANES]
                               * jnp.where(first_half, ecols[0], ecols[1]))
            outs.append(y_pair)
    return outs


def _ssd_rev_kernel(x_ref, b_ref, c_ref, dt_ref, alog_ref, exp_ref, y_ref, state_ref, *, n_chunks):
    step = pl.program_id(1)

    @pl.when(step == 0)
    def _():
        state_ref[...] = jnp.zeros(state_ref.shape, F32)

    outs = _ssd_body(x_ref, b_ref, c_ref, dt_ref, alog_ref, exp_ref, state_ref, reverse=True)

    @pl.when(step < n_chunks - 1)
    def _():
        for k, y in enumerate(outs):
            y_ref[0, :, k * LANES:(k + 1) * LANES] = y


def _ssd_fwd_kernel(x_ref, b_ref, c_ref, dt_ref, alog_ref, exp_ref, yb_ref, z_ref,
                    dskip_ref, gain_ref, o_ref, state_ref, ybuf_ref):
    step = pl.program_id(1)

    @pl.when(step == 0)
    def _():
        state_ref[...] = jnp.zeros(state_ref.shape, F32)

    outs = _ssd_body(x_ref, b_ref, c_ref, dt_ref, alog_ref, exp_ref, state_ref, reverse=False)

    @pl.when(step > 0)
    def _():
        for k, y in enumerate(outs):
            ls = slice(k * LANES, (k + 1) * LANES)
            x = x_ref[0, :, ls].astype(F32)
            z = z_ref[0, :, ls].astype(F32)
            yk = (y + yb_ref[0, :, ls]) + x * dskip_ref[:, ls]
            ybuf_ref[:, ls] = yk * (z * (1.0 / (1.0 + jnp.exp(-z))))
        for g in range(SSD_N_GROUPS):
            gs = slice(g * SSD_GROUP_W, (g + 1) * SSD_GROUP_W)
            yg = ybuf_ref[:, gs]
            ms = jnp.mean(yg * yg, axis=-1, keepdims=True)
            o_ref[0, :, gs] = (yg * lax.rsqrt(ms + EPS) * gain_ref[:, gs]).astype(o_ref.dtype)


def _ssd(xs, bmat, cmat, dt, a_log, d_skip, ssd_norm_g, z):
    b, lp, _ = xs.shape
    n_chunks = lp // SSD_CHUNK
    s = lp - SSD_CHUNK
    nd = dt.shape[-1]
    alog = a_log.reshape(1, nd)
    heads = jnp.arange(D_MODEL, dtype=jnp.int32) // SSD_HEAD_DIM
    lanes = jnp.arange(nd, dtype=jnp.int32)
    expand_f = (lanes[:, None] == heads[None, :]).astype(BF16)
    expand_r = (lanes[:, None] == heads[None, :] + SSD_N_HEADS).astype(BF16)
    dskip = jnp.repeat(d_skip, SSD_HEAD_DIM).reshape(1, D_MODEL)
    gain = ssd_norm_g.reshape(1, D_MODEL)
    state = pltpu.VMEM((SSD_N_GROUPS, SSD_D_STATE, SSD_GROUP_W), F32)

    def chunk_specs(cmap):
        return [
            pl.BlockSpec((1, SSD_CHUNK, D_MODEL), lambda i, c: (i, cmap(c), 0)),
            pl.BlockSpec((1, SSD_CHUNK, SSD_GN), lambda i, c: (i, cmap(c), 0)),
            pl.BlockSpec((1, SSD_CHUNK, SSD_GN), lambda i, c: (i, cmap(c), 0)),
            pl.BlockSpec((1, SSD_CHUNK, nd), lambda i, c: (i, cmap(c), 0)),
            pl.BlockSpec((1, nd), lambda i, c: (0, 0)),
            pl.BlockSpec((nd, D_MODEL), lambda i, c: (0, 0)),
        ]

    rmap = lambda c: n_chunks - 1 - c
    rout = lambda c: jnp.maximum(n_chunks - 2 - c, 0)
    y_rev = pl.pallas_call(
        functools.partial(_ssd_rev_kernel, n_chunks=n_chunks),
        grid=(b, n_chunks),
        in_specs=chunk_specs(rmap),
        out_specs=pl.BlockSpec((1, SSD_CHUNK, D_MODEL), lambda i, c: (i, rout(c), 0)),
        out_shape=jax.ShapeDtypeStruct((b, s, D_MODEL), F32),
        scratch_shapes=[state],
        compiler_params=_params(2, 48),
    )(xs, bmat, cmat, dt, alog, expand_r)

    fmap = lambda c: c
    fout = lambda c: jnp.maximum(c - 1, 0)
    return pl.pallas_call(
        _ssd_fwd_kernel,
        grid=(b, n_chunks),
        in_specs=chunk_specs(fmap) + [
            pl.BlockSpec((1, SSD_CHUNK, D_MODEL), lambda i, c: (i, fout(c), 0)),
            pl.BlockSpec((1, SSD_CHUNK, D_MODEL), lambda i, c: (i, fout(c), 0)),
            pl.BlockSpec((1, D_MODEL), lambda i, c: (0, 0)),
            pl.BlockSpec((1, D_MODEL), lambda i, c: (0, 0)),
        ],
        out_specs=pl.BlockSpec((1, SSD_CHUNK, D_MODEL), lambda i, c: (i, fout(c), 0)),
        out_shape=jax.ShapeDtypeStruct((b, s, D_MODEL), BF16),
        scratch_shapes=[state, pltpu.VMEM((SSD_CHUNK, D_MODEL), F32)],
        compiler_params=_params(2, 48),
    )(xs, bmat, cmat, dt, alog, expand_f, y_rev, z, dskip, gain)


ATT_ONES_ROWS = 16
ATT_DROP_NATS = 25.0


def _attn_kernel(tiles_ref, slopes_ref, q_ref, k_ref, v_ref, kt_ref, vt_ref,
                 lq1_ref, lk1_ref, lq2_ref, lk2_ref, sub_ref, o_ref,
                 vT_ref, vtT_ref, bias_ref, s0_ref, s1_ref, acc_ref, *, tq, tk, n_kt, n_qt, lam_init):
    h = pl.program_id(1)
    i = pl.program_id(2)
    lo = tiles_ref[3 * (h * n_qt + i)]
    hi = tiles_ref[3 * (h * n_qt + i) + 1]
    use_tail = tiles_ref[3 * (h * n_qt + i) + 2]
    slope = slopes_ref[h]
    vrows = ATT_V_DIM + ATT_ONES_ROWS

    @pl.when(i == 0)
    def _():
        for t in range(n_kt):
            vT_ref[t, 0:ATT_V_DIM, :] = v_ref[0, t * tk:(t + 1) * tk, :].astype(F32).T.astype(BF16)
            vT_ref[t, ATT_V_DIM:vrows, :] = jnp.ones((ATT_ONES_ROWS, tk), BF16)
        vtT_ref[0:ATT_V_DIM, :] = vt_ref[...].astype(F32).T.astype(BF16)
        vtT_ref[ATT_V_DIM:vrows, :] = jnp.ones((ATT_ONES_ROWS, LANES), BF16)
        srel = slope * (lax.broadcasted_iota(jnp.int32, (tk, tq), 0)
                        - lax.broadcasted_iota(jnp.int32, (tk, tq), 1)).astype(F32)
        bias_ref[0] = -srel
        bias_ref[1] = srel
        bias_ref[2] = jnp.abs(srel)

    q = q_ref[0]
    lane = lax.broadcasted_iota(jnp.int32, q.shape, 1)
    zero = jnp.zeros_like(q)
    qs = (jnp.where(lane < ATT_HALF_DIM, q, zero), jnp.where(lane >= ATT_HALF_DIM, q, zero))

    def scores(k, bias_tile):
        return [lax.dot_general(k, qs[e], (((1,), (1,)), ((), ())),
                                preferred_element_type=F32) - bias_tile for e in range(2)]

    def probs(s, m):
        return jnp.exp2(s - m).astype(BF16)

    q0 = (i * tq).astype(F32)
    n_tail = kt_ref.shape[0]

    def with_tail():
        tail_valid = lax.broadcasted_iota(jnp.int32, (n_tail, 1), 0) < N_META
        c_tail = slope * (q0 + N_META)
        ms = []
        for e, s in enumerate(scores(kt_ref[...], bias_ref[0, 0:n_tail, :])):
            s = jnp.where(tail_valid, s, -jnp.inf)
            m = jnp.max(s, axis=0, keepdims=True)
            acc_ref[e] = jnp.dot(vtT_ref[...], probs(s, m), preferred_element_type=F32)
            ms.append(m - c_tail)
        return tuple(ms)

    def without_tail():
        acc_ref[...] = jnp.zeros(acc_ref.shape, F32)
        return (jnp.full((1, tq), -jnp.inf, F32),) * 2

    ms = lax.cond(use_tail == 1, with_tail, without_tail)

    def stage_a(t, dst_ref):
        start = pl.multiple_of(t * tk, tk)
        side = jnp.where(t < i, 0, jnp.where(t > i, 1, 2))
        c = slope * jnp.abs(start.astype(F32) - q0)
        ss = scores(k_ref[0, pl.ds(start, tk), :], bias_ref[side])
        for e in range(2):
            dst_ref[e] = ss[e]
        return tuple(jnp.max(ss[e], axis=0, keepdims=True) - c for e in range(2)) + (c,)

    def stage_b(t, src_ref, ms, cmax):
        vt_aug = vT_ref[t]
        out = []
        for e in range(2):
            m_new = jnp.maximum(ms[e], cmax[e])
            alpha = jnp.exp2(ms[e] - m_new)
            p = probs(src_ref[e], m_new + cmax[2])
            acc_ref[e] = alpha * acc_ref[e] + jnp.dot(vt_aug, p, preferred_element_type=F32)
            out.append(m_new)
        return tuple(out)

    def two_trips(u, carry):
        ms, cmax0 = carry
        t = lo + 2 * u
        cmax1 = stage_a(t + 1, s1_ref)
        ms = stage_b(t, s0_ref, ms, cmax0)
        cmax0 = stage_a(t + 2, s0_ref)
        ms = stage_b(t + 1, s1_ref, ms, cmax1)
        return ms, cmax0

    n_tiles = hi - lo
    n_two = (n_tiles - 1) // 2
    ms, cmax0 = lax.fori_loop(0, n_two, two_trips, (ms, stage_a(lo, s0_ref)))
    t_last = lo + 2 * n_two

    def finish_two(ms, cmax0):
        cmax1 = stage_a(t_last + 1, s1_ref)
        stage_b(t_last + 1, s1_ref, stage_b(t_last, s0_ref, ms, cmax0), cmax1)

    def finish_one(ms, cmax0):
        stage_b(t_last, s0_ref, ms, cmax0)

    lax.cond(n_tiles - 2 * n_two == 2, finish_two, finish_one, ms, cmax0)
    accs = (acc_ref[0], acc_ref[1])

    lam = (jnp.exp(jnp.sum(lq1_ref[...] * lk1_ref[...], axis=-1, keepdims=True))
           - jnp.exp(jnp.sum(lq2_ref[...] * lk2_ref[...], axis=-1, keepdims=True))
           + lam_init)
    o1, o2 = (a[0:ATT_V_DIM] / a[ATT_V_DIM:ATT_V_DIM + 1] for a in accs)
    o = o1 - lam * o2
    msq = jnp.mean(o * o, axis=0, keepdims=True)
    o = (o * lax.rsqrt(msq + EPS) * sub_ref[...]) * (1.0 - lam_init)
    o_ref[0] = o.T.astype(o_ref.dtype)


def _attention(q, k, v, k_tail, v_tail, q_gain, k_gain, lambdas, subln_g, layer_idx):
    b, s, _ = q.shape
    tq = tk = min(512, s)
    n_qt, n_kt = s // tq, s // tk
    lam_init = 0.8 - 0.6 * math.exp(-0.3 * layer_idx)
    slopes = jnp.exp2(-8.0 * jnp.arange(1, ATT_N_HEADS + 1, dtype=F32) / ATT_N_HEADS)
    bound = 1.01 * ATT_HALF_DIM ** 0.5 * jnp.max(jnp.abs(q_gain)) * jnp.max(jnp.abs(k_gain))
    reach = jnp.ceil((2.0 * bound + ATT_DROP_NATS) / slopes)
    reach = jnp.minimum(reach, float(2 * s)).astype(jnp.int32)[:, None]
    q_first = (jnp.arange(n_qt, dtype=jnp.int32) * tq)[None, :]
    lo = jnp.maximum(q_first - reach, 0) // tk
    hi = jnp.minimum((q_first + tq - 1 + reach) // tk + 1, n_kt)
    use_tail = (q_first + 1 <= reach).astype(jnp.int32)
    tiles = jnp.stack([lo, hi, use_tail], axis=-1).reshape(-1).astype(jnp.int32)
    kern = functools.partial(_attn_kernel, tq=tq, tk=tk, n_kt=n_kt, n_qt=n_qt, lam_init=lam_init)
    vec = lambda x: x.reshape(1, -1)
    small = lambda n: pl.BlockSpec((1, n), lambda bi, h, i: (0, 0))
    vrows = ATT_V_DIM + ATT_ONES_ROWS
    return pl.pallas_call(
        kern,
        grid=(b, ATT_N_HEADS, n_qt),
        in_specs=[
            pl.BlockSpec(memory_space=pltpu.SMEM),
            pl.BlockSpec(memory_space=pltpu.SMEM),
            pl.BlockSpec((1, tq, LANES), lambda bi, h, i: (bi, i, h)),
            pl.BlockSpec((1, s, LANES), lambda bi, h, i: (bi, 0, h)),
            pl.BlockSpec((1, s, LANES), lambda bi, h, i: (bi, 0, h)),
            pl.BlockSpec((LANES, LANES), lambda bi, h, i: (0, h)),
            pl.BlockSpec((LANES, LANES), lambda bi, h, i: (0, h)),
            small(ATT_HALF_DIM), small(ATT_HALF_DIM), small(ATT_HALF_DIM), small(ATT_HALF_DIM),
            pl.BlockSpec((ATT_V_DIM, 1), lambda bi, h, i: (0, 0)),
        ],
        out_specs=pl.BlockSpec((1, tq, LANES), lambda bi, h, i: (bi, i, h)),
        out_shape=jax.ShapeDtypeStruct((b, s, ATT_WIDTH), BF16),
        scratch_shapes=[
            pltpu.VMEM((n_kt, vrows, tk), BF16),
            pltpu.VMEM((vrows, LANES), BF16),
            pltpu.VMEM((3, tk, tq), F32),
            pltpu.VMEM((2, tk, tq), F32),
            pltpu.VMEM((2, tk, tq), F32),
            pltpu.VMEM((2, vrows, tq), F32),
        ],
        compiler_params=_params(3, 48),
    )(tiles, slopes * LOG2E, q, k, v, k_tail, v_tail, *[vec(x) for x in lambdas],
      subln_g.reshape(-1, 1))


def _proj(h_in, w, n, dtype, ep=None, **kw):
    return _matmul([h_in], [(w, 0)], [(0, 0)], ep or _ep_plain, n, dtype, 1024, min(1024, n), 52, **kw)


def _prepare(meta_tokens, wts):
    bf = lambda w: w.astype(BF16)
    o1 = D_MODEL
    o2 = o1 + SSD_CONV_CH
    o3 = o2 + 2 * SSD_N_HEADS
    o4 = o3 + ATT_WIDTH
    o5 = o4 + ATT_WIDTH
    w_in = wts["w_in"]
    p = dict(wts)
    p["w_z"], p["w_xbc"], p["w_dt"] = bf(w_in[:, :o1]), bf(w_in[:, o1:o2]), bf(w_in[:, o2:o3])
    p["w_q"], p["w_k"], p["w_v"] = bf(w_in[:, o3:o4]), bf(w_in[:, o4:o5]), bf(w_in[:, o5:])
    for name in ("w_branch_gate", "w_ssd_out", "w_att_out", "w_o", "w_down"):
        p[name] = bf(wts[name])
    p["w_gate_up"] = bf(wts["norm_ffn_g"][:, None] * wts["w_gate_up"])
    ones_bd = (jnp.arange(256)[:, None] // ATT_HALF_DIM
               == jnp.arange(256)[None, :] // ATT_HALF_DIM).astype(BF16)
    p["qk_norm"] = dict(
        q=dict(rows=[(jnp.tile(wts["q_norm_g"] * (ATT_HALF_DIM ** -0.5 * LOG2E),
                               2 * ATT_N_HEADS).reshape(1, ATT_WIDTH), 0)], consts=[ones_bd]),
        k=dict(rows=[(jnp.tile(wts["k_norm_g"], 2 * ATT_N_HEADS).reshape(1, ATT_WIDTH), 0)],
               consts=[ones_bd]))
    hn_meta = _rmsnorm(meta_tokens, wts["norm_mix_g"], N_META)
    tail_pad = ((0, LANES - N_META), (0, 0))
    p["xbc_meta"] = _proj(hn_meta, p["w_xbc"], SSD_CONV_CH, BF16)
    p["dt_raw_meta"] = _proj(hn_meta, p["w_dt"], 2 * SSD_N_HEADS, F32)
    p["k_tail"] = jnp.pad(_proj(hn_meta, p["w_k"], ATT_WIDTH, BF16, ep=_ep_qknorm,
                                **p["qk_norm"]["k"]), tail_pad)
    p["v_tail"] = jnp.pad(_proj(hn_meta, p["w_v"], ATT_WIDTH, BF16), tail_pad)
    return p


def _layer(x, layer_idx, p):
    b, s, d = x.shape
    m = b * s
    x2 = x.reshape(m, d)
    hn = _rmsnorm(x2, p["norm_mix_g"], 256)

    z = _proj(hn, p["w_z"], D_MODEL, BF16)
    xbc = _proj(hn, p["w_xbc"], SSD_CONV_CH, BF16)
    dt_raw = _proj(hn, p["w_dt"], 2 * SSD_N_HEADS, F32)
    q = _proj(hn, p["w_q"], ATT_WIDTH, BF16, ep=_ep_qknorm, **p["qk_norm"]["q"])
    k = _proj(hn, p["w_k"], ATT_WIDTH, BF16, ep=_ep_qknorm, **p["qk_norm"]["k"])
    v = _proj(hn, p["w_v"], ATT_WIDTH, BF16)

    xs, bmat, cmat, dt = _conv_dt(
        xbc.reshape(b, s, SSD_CONV_CH), p["xbc_meta"], p["conv_w"], p["conv_b"],
        dt_raw.reshape(b, s, 2 * SSD_N_HEADS), p["dt_raw_meta"], p["dt_bias"].reshape(-1))
    y_ssd = _ssd(xs, bmat, cmat, dt, p["a_log"].reshape(-1), p["d_skip"],
                 p["ssd_norm_g"], z.reshape(b, s, D_MODEL))

    y_att = _attention(
        q.reshape(b, s, ATT_WIDTH), k.reshape(b, s, ATT_WIDTH), v.reshape(b, s, ATT_WIDTH),
        p["k_tail"], p["v_tail"], p["q_norm_g"], p["k_norm_g"],
        (p["lambda_q1"], p["lambda_k1"], p["lambda_q2"], p["lambda_k2"]),
        p["subln_g"], layer_idx)

    gates = _matmul([hn], [(p["w_branch_gate"], 0)], [(0, 0)], _ep_sigmoid_bias,
                    2 * D_MODEL, BF16, 1024, 1024, 52,
                    rows=[(p["b_branch_gate"].reshape(1, -1), 0)])
    bn_mix = 512
    mixed = _matmul([y_ssd.reshape(m, D_MODEL), y_att.reshape(m, ATT_WIDTH)],
                    [(p["w_ssd_out"], 0), (p["w_att_out"], 0)],
                    [(0, 0), (1, 1)], _ep_gated_merge, D_MODEL, BF16, 1024, bn_mix, 56,
                    tiles=[(gates, 0), (gates, D_MODEL // bn_mix)])
    h2, h2_bf, h2_sumsq = _matmul([mixed], [(p["w_o"], 0)], [(0, 0)], _ep_residual,
                                  D_MODEL, F32, 1024, 512, 52, tiles=[(x2, 0)], emit_stats=True)

    bn_ff = 256
    act = _matmul([h2_bf], [(p["w_gate_up"], 0), (p["w_gate_up"], D_FF // bn_ff)],
                  [(0, 0), (0, 1)], _ep_swiglu, D_FF, BF16, 2048, bn_ff, 52,
                  row_blocks=[h2_sumsq])
    out = _matmul([act], [(p["w_down"], 0)], [(0, 0)], _ep_residual,
                  D_MODEL, F32, 512, 512, 56, tiles=[(h2, 0)])
    return out.reshape(b, s, d)


def kernel(x_prompt, x_sample, meta_tokens, norm_mix_g, w_in, conv_w, conv_b, dt_bias, a_log, d_skip, ssd_norm_g, q_norm_g, k_norm_g, lambda_q1, lambda_k1, lambda_q2, lambda_k2, subln_g, w_branch_gate, b_branch_gate, w_ssd_out, w_att_out, w_o, norm_ffn_g, w_gate_up, w_down):
    names = ("norm_mix_g", "w_in", "conv_w", "conv_b", "dt_bias", "a_log", "d_skip",
             "ssd_norm_g", "q_norm_g", "k_norm_g", "lambda_q1", "lambda_k1", "lambda_q2",
             "lambda_k2", "subln_g", "w_branch_gate", "b_branch_gate", "w_ssd_out",
             "w_att_out", "w_o", "norm_ffn_g", "w_gate_up", "w_down")
    stacked = (norm_mix_g, w_in, conv_w, conv_b, dt_bias, a_log, d_skip, ssd_norm_g,
               q_norm_g, k_norm_g, lambda_q1, lambda_k1, lambda_q2, lambda_k2, subln_g,
               w_branch_gate, b_branch_gate, w_ssd_out, w_att_out, w_o, norm_ffn_g,
               w_gate_up, w_down)
    assert all(w.shape[0] == 1 for w in stacked), "single-layer encoder"
    p = _prepare(meta_tokens, {n: w[0] for n, w in zip(names, stacked)})
    return (_layer(x_prompt, 0, p), _layer(x_sample, 0, p))
```

```python
import functools
import math

import jax
import jax.numpy as jnp
from jax import lax
from jax.experimental import pallas as pl
from jax.experimental.pallas import tpu as pltpu

F32 = jnp.float32
BF16 = jnp.bfloat16

D_MODEL = 4096
N_META = 16
EPS = 1e-6
SSD_HEAD_DIM = 64
SSD_N_HEADS = D_MODEL // SSD_HEAD_DIM
SSD_N_GROUPS = 8
SSD_HEADS_PER_GROUP = SSD_N_HEADS // SSD_N_GROUPS
SSD_D_STATE = 128
SSD_CONV = 5
SSD_CHUNK = 128
SSD_GN = SSD_N_GROUPS * SSD_D_STATE
SSD_CONV_CH = D_MODEL + 2 * SSD_GN
SSD_GROUP_W = SSD_HEADS_PER_GROUP * SSD_HEAD_DIM
ATT_N_HEADS = 16
ATT_HALF_DIM = 64
ATT_V_DIM = 128
ATT_WIDTH = ATT_N_HEADS * ATT_V_DIM
D_FF = 11008
LANES = 128
HALO = 16
MIB = 1024 * 1024
LOG2E = 1.4426950408889634


def _params(n_axes, vmem_mib):
    return pltpu.CompilerParams(
        dimension_semantics=("arbitrary",) * n_axes,
        vmem_limit_bytes=vmem_mib * MIB)


def _rmsnorm_kernel(x_ref, g_ref, o_ref):
    x = x_ref[...]
    ms = jnp.mean(x * x, axis=-1, keepdims=True)
    o_ref[...] = (x * lax.rsqrt(ms + EPS) * g_ref[...]).astype(o_ref.dtype)


def _rmsnorm(x, g, bm):
    m, d = x.shape
    bm = min(bm, m)
    return pl.pallas_call(
        _rmsnorm_kernel,
        grid=(m // bm,),
        in_specs=[pl.BlockSpec((bm, d), lambda i: (i, 0)),
                  pl.BlockSpec((1, d), lambda i: (0, 0))],
        out_specs=pl.BlockSpec((bm, d), lambda i: (i, 0)),
        out_shape=jax.ShapeDtypeStruct((m, d), BF16),
        compiler_params=_params(1, 32),
    )(x, g.reshape(1, d))


def _mm_kernel(*refs, dots, n_lhs, n_rhs, n_tiles, n_rows, n_consts, epilogue, emit_stats,
               head_major):
    lhs = refs[:n_lhs]
    rhs = refs[n_lhs:n_lhs + n_rhs]
    p = n_lhs + n_rhs
    tiles = refs[p:p + n_tiles]
    p += n_tiles
    rows = refs[p:p + n_rows]
    p += n_rows
    consts = refs[p:p + n_consts]
    p += n_consts
    o_ref = refs[p]
    accs = [jnp.dot(lhs[a][...], rhs[b][...], preferred_element_type=F32)
            for a, b in dots]
    out = epilogue(accs, [t[...] for t in tiles], [r[...] for r in rows],
                   [c[...] for c in consts])
    if head_major:
        for hh in range(o_ref.shape[1]):
            o_ref[0, hh] = out[:, hh * LANES:(hh + 1) * LANES].astype(o_ref.dtype)
    else:
        o_ref[...] = out.astype(o_ref.dtype)
    if emit_stats:
        obf_ref, ss_ref = refs[p + 1], refs[p + 2]
        obf_ref[...] = out.astype(obf_ref.dtype)

        @pl.when(pl.program_id(1) == 0)
        def _():
            ss_ref[...] = jnp.zeros(ss_ref.shape, F32)

        ss_ref[...] += jnp.broadcast_to(jnp.sum(out * out, axis=1, keepdims=True), ss_ref.shape)


def _matmul(lhs, rhs, dots, epilogue, n_out, out_dtype, bm, bn, vmem_mib,
            tiles=(), rows=(), consts=(), row_blocks=(), emit_stats=False, seq_len=None):
    m = lhs[0].shape[0]
    bm = min(bm, m)
    in_specs = []
    args = []
    for a in lhs:
        in_specs.append(pl.BlockSpec((bm, a.shape[1]), lambda i, j: (i, 0)))
        args.append(a)
    for w, off in rhs:
        in_specs.append(pl.BlockSpec((w.shape[0], bn), lambda i, j, off=off: (0, j + off)))
        args.append(w)
    for t, off in tiles:
        in_specs.append(pl.BlockSpec((bm, bn), lambda i, j, off=off: (i, j + off)))
        args.append(t)
    for t in row_blocks:
        in_specs.append(pl.BlockSpec((bm, t.shape[1]), lambda i, j: (i, 0)))
        args.append(t)
    for r, off in rows:
        in_specs.append(pl.BlockSpec((1, bn), lambda i, j, off=off: (0, j + off)))
        args.append(r)
    for c in consts:
        in_specs.append(pl.BlockSpec(c.shape, lambda i, j, nd=c.ndim: (0,) * nd))
        args.append(c)
    out_specs = pl.BlockSpec((bm, bn), lambda i, j: (i, j))
    out_shape = jax.ShapeDtypeStruct((m, n_out), out_dtype)
    if seq_len is not None:
        assert seq_len % bm == 0 and bn % LANES == 0
        per_seq = seq_len // bm
        out_specs = pl.BlockSpec((1, bn // LANES, bm, LANES),
                                 lambda i, j: (i // per_seq, j, i % per_seq, 0))
        out_shape = jax.ShapeDtypeStruct((m // seq_len, n_out // LANES, seq_len, LANES), out_dtype)
    if emit_stats:
        out_specs = [out_specs, pl.BlockSpec((bm, bn), lambda i, j: (i, j)),
                     pl.BlockSpec((bm, LANES), lambda i, j: (i, 0))]
        out_shape = [out_shape, jax.ShapeDtypeStruct((m, n_out), BF16),
                     jax.ShapeDtypeStruct((m, LANES), F32)]
    kern = functools.partial(
        _mm_kernel, dots=tuple(dots), n_lhs=len(lhs), n_rhs=len(rhs),
        n_tiles=len(tiles) + len(row_blocks), n_rows=len(rows), n_consts=len(consts),
        epilogue=epilogue, emit_stats=emit_stats, head_major=seq_len is not None)
    return pl.pallas_call(
        kern,
        grid=(m // bm, n_out // bn),
        in_specs=in_specs,
        out_specs=out_specs,
        out_shape=out_shape,
        compiler_params=_params(2, vmem_mib),
    )(*args)


def _ep_plain(accs, tiles, rows, consts):
    return accs[0]


def _ep_qknorm(accs, tiles, rows, consts):
    acc = accs[0]
    ones_bd = consts[0]
    width = ones_bd.shape[0]
    outs = []
    for c in range(acc.shape[1] // width):
        a = acc[:, c * width:(c + 1) * width]
        ss = jnp.dot((a * a).astype(BF16), ones_bd, preferred_element_type=F32)
        outs.append(a * lax.rsqrt(ss * (1.0 / ATT_HALF_DIM) + EPS))
    return jnp.concatenate(outs, axis=1) * rows[0]


def _ep_sigmoid_bias(accs, tiles, rows, consts):
    return 1.0 / (1.0 + jnp.exp(-(accs[0] + rows[0])))


def _ep_gated_merge(accs, tiles, rows, consts):
    return tiles[0].astype(F32) * accs[0] + tiles[1].astype(F32) * accs[1]


def _ep_residual(accs, tiles, rows, consts):
    return tiles[0] + accs[0]


def _ep_swiglu(accs, tiles, rows, consts):
    inv_rms = lax.rsqrt(tiles[0][:, 0:1] * (1.0 / D_MODEL) + EPS)
    g = accs[0] * inv_rms
    return (g * (1.0 / (1.0 + jnp.exp(-g)))) * (accs[1] * inv_rms)


def _conv_kernel(cur_ref, prev_ref, next_ref, meta_ref, w_ref, b_ref, shift_ref,
                 dtr_ref, dtm_ref, dtb_ref,
                 xs_ref, bm_ref, cm_ref, dt_ref, ext_ref, *, n_chunks):
    c = pl.program_id(1)
    pad = SSD_CHUNK - N_META
    ch = ext_ref.shape[1]
    dtype = ext_ref.dtype

    @pl.when(c == 0)
    def _():
        ext_ref[0:HALO + pad, :] = jnp.zeros((HALO + pad, ch), dtype)
        ext_ref[HALO + pad:HALO + SSD_CHUNK, :] = meta_ref[...]

    @pl.when(c == 1)
    def _():
        ext_ref[0:HALO, :] = meta_ref[...]

    @pl.when(c > 1)
    def _():
        ext_ref[0:HALO, :] = prev_ref[0]

    @pl.when(c > 0)
    def _():
        ext_ref[HALO:HALO + SSD_CHUNK, :] = cur_ref[0]

    @pl.when(c < n_chunks - 1)
    def _():
        ext_ref[HALO + SSD_CHUNK:, :] = next_ref[0]

    @pl.when(c == n_chunks - 1)
    def _():
        ext_ref[HALO + SSD_CHUNK:, :] = jnp.zeros((HALO, ch), dtype)

    half = (SSD_CONV - 1) // 2

    def conv_silu(lo, hi):
        ext = ext_ref[:, lo:hi]
        shifted = jnp.dot(shift_ref[...], ext, preferred_element_type=F32)
        acc = None
        idx = 0
        for k in range(SSD_CONV):
            if k == half:
                x = ext[HALO:HALO + SSD_CHUNK].astype(F32)
            else:
                x = shifted[idx * SSD_CHUNK:(idx + 1) * SSD_CHUNK]
                idx += 1
            term = x * w_ref[k:k + 1, lo:hi]
            acc = term if acc is None else acc + term
        acc = acc + b_ref[:, lo:hi]
        return acc * (1.0 / (1.0 + jnp.exp(-acc)))

    blk = 1024
    for s in range(D_MODEL // blk):
        xs_ref[0, :, s * blk:(s + 1) * blk] = conv_silu(s * blk, (s + 1) * blk).astype(xs_ref.dtype)
    bm_ref[0] = conv_silu(D_MODEL, D_MODEL + SSD_GN).astype(bm_ref.dtype)
    cm_ref[0] = conv_silu(D_MODEL + SSD_GN, SSD_CONV_CH).astype(cm_ref.dtype)

    @pl.when(c == 0)
    def _():
        xs_ref[0, 0:pad, :] = jnp.zeros((pad, D_MODEL), xs_ref.dtype)
        bm_ref[0, 0:pad, :] = jnp.zeros((pad, SSD_GN), bm_ref.dtype)
        cm_ref[0, 0:pad, :] = jnp.zeros((pad, SSD_GN), cm_ref.dtype)

    row = lax.broadcasted_iota(jnp.int32, (SSD_CHUNK, dt_ref.shape[2]), 0)
    valid = jnp.logical_or(c > 0, row >= pad)

    raw_real = dtr_ref[0]
    raw_meta = jnp.concatenate(
        [jnp.zeros((pad, raw_real.shape[1]), F32), dtm_ref[...]], axis=0)
    raw = jnp.where(c > 0, raw_real, raw_meta) + dtb_ref[...]
    sp = jnp.maximum(raw, 0.0) + jnp.log1p(jnp.exp(-jnp.abs(raw)))
    dt_ref[0] = jnp.where(valid, sp, 0.0)


def _conv_dt(xbc, xbc_meta, conv_w, conv_b, dt_raw, dt_raw_meta, dt_bias):
    b, s, ch = xbc.shape
    n_chunks = s // SSD_CHUNK + 1
    lp = n_chunks * SSD_CHUNK
    hb = SSD_CHUNK // HALO
    n_halo = s // HALO
    nd = dt_raw.shape[-1]
    kern = functools.partial(_conv_kernel, n_chunks=n_chunks)
    n_ext = 2 * HALO + SSD_CHUNK
    half = (SSD_CONV - 1) // 2
    taps = jnp.array([k - half for k in range(SSD_CONV) if k != half], jnp.int32)
    src = HALO + jnp.arange(SSD_CHUNK, dtype=jnp.int32)[None, :] + taps[:, None]
    shift = (src.reshape(-1, 1) == jnp.arange(n_ext, dtype=jnp.int32)[None, :]).astype(BF16)
    return pl.pallas_call(
        kern,
        grid=(b, n_chunks),
        in_specs=[
            pl.BlockSpec((1, SSD_CHUNK, ch), lambda i, c: (i, jnp.maximum(c - 1, 0), 0)),
            pl.BlockSpec((1, HALO, ch), lambda i, c: (i, jnp.maximum((c - 1) * hb - 1, 0), 0)),
            pl.BlockSpec((1, HALO, ch), lambda i, c: (i, jnp.minimum(c * hb, n_halo - 1), 0)),
            pl.BlockSpec((N_META, ch), lambda i, c: (0, 0)),
            pl.BlockSpec((SSD_CONV, ch), lambda i, c: (0, 0)),
            pl.BlockSpec((1, ch), lambda i, c: (0, 0)),
            pl.BlockSpec(((SSD_CONV - 1) * SSD_CHUNK, n_ext), lambda i, c: (0, 0)),
            pl.BlockSpec((1, SSD_CHUNK, nd), lambda i, c: (i, jnp.maximum(c - 1, 0), 0)),
            pl.BlockSpec((N_META, nd), lambda i, c: (0, 0)),
            pl.BlockSpec((1, nd), lambda i, c: (0, 0)),
        ],
        out_specs=[
            pl.BlockSpec((1, SSD_CHUNK, D_MODEL), lambda i, c: (i, c, 0)),
            pl.BlockSpec((1, SSD_CHUNK, SSD_GN), lambda i, c: (i, c, 0)),
            pl.BlockSpec((1, SSD_CHUNK, SSD_GN), lambda i, c: (i, c, 0)),
            pl.BlockSpec((1, SSD_CHUNK, nd), lambda i, c: (i, c, 0)),
        ],
        out_shape=[
            jax.ShapeDtypeStruct((b, lp, D_MODEL), BF16),
            jax.ShapeDtypeStruct((b, lp, SSD_GN), BF16),
            jax.ShapeDtypeStruct((b, lp, SSD_GN), BF16),
            jax.ShapeDtypeStruct((b, lp, nd), F32),
        ],
        scratch_shapes=[pltpu.VMEM((n_ext, ch), BF16)],
        compiler_params=_params(2, 40),
    )(xbc, xbc, xbc, xbc_meta, conv_w, conv_b.reshape(1, ch), shift,
      dt_raw, dt_raw_meta, dt_bias.reshape(1, nd))


def _split3(x):
    hi = x.astype(BF16)
    r = x - hi.astype(F32)
    mid = r.astype(BF16)
    lo = (r - mid.astype(F32)).astype(BF16)
    return hi, mid, lo


def _ssd_body(x_ref, b_ref, c_ref, dt_ref, alog_ref, exp_ref, state_ref, *, reverse):
    t = SSD_CHUNK
    lane0 = SSD_N_HEADS if reverse else 0
    dt = dt_ref[0]
    a = dt * (-jnp.exp(alog_ref[...]) * LOG2E)
    ri = lax.broadcasted_iota(jnp.int32, (t, t), 0)
    ci = lax.broadcasted_iota(jnp.int32, (t, t), 1)
    causal = (ci >= ri) if reverse else (ci <= ri)
    tri = causal.astype(BF16)
    a3 = _split3(a)
    cs = sum(jnp.dot(tri, p, preferred_element_type=F32) for p in a3)
    cs_t = cs.T
    dt_t = dt.T
    edge = 0 if reverse else t - 1
    cs_edge = cs[edge:edge + 1, :]
    state_in = dt * jnp.exp2(cs_edge - cs)
    edge_parts = jnp.concatenate(
        [p.astype(F32) for p in _split3(jnp.exp2(cs_edge))]
        + [jnp.zeros((HALO - 3, LANES), F32)], axis=0)
    stacked = jnp.concatenate([state_in, edge_parts], axis=0).astype(BF16)
    expanded = jnp.dot(stacked, exp_ref[...], preferred_element_type=F32)
    chunk_decay_all = expanded[t:t + 1] + expanded[t + 1:t + 2] + expanded[t + 2:t + 3]
    first_half = lax.broadcasted_iota(jnp.int32, (t, LANES), 1) < SSD_HEAD_DIM
    outs = []
    for g in range(SSD_N_GROUPS):
        gs = slice(g * SSD_GROUP_W, (g + 1) * SSD_GROUP_W)
        ns = slice(g * SSD_D_STATE, (g + 1) * SSD_D_STATE)
        bg = b_ref[0, :, ns]
        cg = c_ref[0, :, ns]
        cb = lax.dot_general(cg, bg, (((1,), (1,)), ((), ())), preferred_element_type=F32)
        cbm = jnp.where(causal, cb, 0.0)
        xg = x_ref[0, :, gs]
        x_in = (xg.astype(F32) * expanded[0:t, gs]).astype(BF16)
        new_state = lax.dot_general(bg, x_in, (((0,), (0,)), ((), ())),
                                    preferred_element_type=F32)
        prev = state_ref[g]
        y_off = jnp.dot(cg, prev.astype(BF16), preferred_element_type=F32)
        state_ref[g] = prev * chunk_decay_all[:, gs] + new_state
        for jp in range(SSD_HEADS_PER_GROUP // 2):
            xp = xg[:, jp * LANES:(jp + 1) * LANES]
            zero = jnp.zeros_like(xp)
            rhs = jnp.concatenate([jnp.where(first_half, xp, zero),
                                   jnp.where(first_half, zero, xp)], axis=0)
            mats, ecols = [], []
            for e in range(2):
                hl = lane0 + g * SSD_HEADS_PER_GROUP + 2 * jp + e
                colb = jnp.broadcast_to(cs[:, hl:hl + 1], (t, t))
                seg = jnp.minimum(colb - cs_t[hl:hl + 1, :], 0.0)
                mats.append((cbm * jnp.exp2(seg) * dt_t[hl:hl + 1, :]).astype(BF16))
                ecols.append(jnp.exp2(colb))
            y_diag = jnp.dot(jnp.concatenate(mats, axis=1), rhs, preferred_element_type=F32)
            y_pair = y_diag + (y_off[:, jp * LANES:(jp + 1) * LANES]
                               * jnp.where(first_half, ecols[0], ecols[1]))
            outs.append(y_pair)
    return outs


def _ssd_rev_kernel(x_ref, b_ref, c_ref, dt_ref, alog_ref, exp_ref, y_ref, state_ref, *, n_chunks):
    step = pl.program_id(1)

    @pl.when(step == 0)
    def _():
        state_ref[...] = jnp.zeros(state_ref.shape, F32)

    outs = _ssd_body(x_ref, b_ref, c_ref, dt_ref, alog_ref, exp_ref, state_ref, reverse=True)

    @pl.when(step < n_chunks - 1)
    def _():
        for k, y in enumerate(outs):
            y_ref[0, :, k * LANES:(k + 1) * LANES] = y


def _ssd_fwd_kernel(x_ref, b_ref, c_ref, dt_ref, alog_ref, exp_ref, yb_ref, z_ref,
                    dskip_ref, gain_ref, o_ref, state_ref, ybuf_ref):
    step = pl.program_id(1)

    @pl.when(step == 0)
    def _():
        state_ref[...] = jnp.zeros(state_ref.shape, F32)

    outs = _ssd_body(x_ref, b_ref, c_ref, dt_ref, alog_ref, exp_ref, state_ref, reverse=False)

    @pl.when(step > 0)
    def _():
        for k, y in enumerate(outs):
            ls = slice(k * LANES, (k + 1) * LANES)
            x = x_ref[0, :, ls].astype(F32)
            z = z_ref[0, :, ls].astype(F32)
            yk = (y + yb_ref[0, :, ls]) + x * dskip_ref[:, ls]
            ybuf_ref[:, ls] = yk * (z * (1.0 / (1.0 + jnp.exp(-z))))
        for g in range(SSD_N_GROUPS):
            gs = slice(g * SSD_GROUP_W, (g + 1) * SSD_GROUP_W)
            yg = ybuf_ref[:, gs]
            ms = jnp.mean(yg * yg, axis=-1, keepdims=True)
            o_ref[0, :, gs] = (yg * lax.rsqrt(ms + EPS) * gain_ref[:, gs]).astype(o_ref.dtype)


def _ssd(xs, bmat, cmat, dt, a_log, d_skip, ssd_norm_g, z):
    b, lp, _ = xs.shape
    n_chunks = lp // SSD_CHUNK
    s = lp - SSD_CHUNK
    nd = dt.shape[-1]
    alog = a_log.reshape(1, nd)
    heads = jnp.arange(D_MODEL, dtype=jnp.int32) // SSD_HEAD_DIM
    lanes = jnp.arange(nd, dtype=jnp.int32)
    expand_f = (lanes[:, None] == heads[None, :]).astype(BF16)
    expand_r = (lanes[:, None] == heads[None, :] + SSD_N_HEADS).astype(BF16)
    dskip = jnp.repeat(d_skip, SSD_HEAD_DIM).reshape(1, D_MODEL)
    gain = ssd_norm_g.reshape(1, D_MODEL)
    state = pltpu.VMEM((SSD_N_GROUPS, SSD_D_STATE, SSD_GROUP_W), F32)

    def chunk_specs(cmap):
        return [
            pl.BlockSpec((1, SSD_CHUNK, D_MODEL), lambda i, c: (i, cmap(c), 0)),
            pl.BlockSpec((1, SSD_CHUNK, SSD_GN), lambda i, c: (i, cmap(c), 0)),
            pl.BlockSpec((1, SSD_CHUNK, SSD_GN), lambda i, c: (i, cmap(c), 0)),
            pl.BlockSpec((1, SSD_CHUNK, nd), lambda i, c: (i, cmap(c), 0)),
            pl.BlockSpec((1, nd), lambda i, c: (0, 0)),
            pl.BlockSpec((nd, D_MODEL), lambda i, c: (0, 0)),
        ]

    rmap = lambda c: n_chunks - 1 - c
    rout = lambda c: jnp.maximum(n_chunks - 2 - c, 0)
    y_rev = pl.pallas_call(
        functools.partial(_ssd_rev_kernel, n_chunks=n_chunks),
        grid=(b, n_chunks),
        in_specs=chunk_specs(rmap),
        out_specs=pl.BlockSpec((1, SSD_CHUNK, D_MODEL), lambda i, c: (i, rout(c), 0)),
        out_shape=jax.ShapeDtypeStruct((b, s, D_MODEL), F32),
        scratch_shapes=[state],
        compiler_params=_params(2, 48),
    )(xs, bmat, cmat, dt, alog, expand_r)

    fmap = lambda c: c
    fout = lambda c: jnp.maximum(c - 1, 0)
    return pl.pallas_call(
        _ssd_fwd_kernel,
        grid=(b, n_chunks),
        in_specs=chunk_specs(fmap) + [
            pl.BlockSpec((1, SSD_CHUNK, D_MODEL), lambda i, c: (i, fout(c), 0)),
            pl.BlockSpec((1, SSD_CHUNK, D_MODEL), lambda i, c: (i, fout(c), 0)),
            pl.BlockSpec((1, D_MODEL), lambda i, c: (0, 0)),
            pl.BlockSpec((1, D_MODEL), lambda i, c: (0, 0)),
        ],
        out_specs=pl.BlockSpec((1, SSD_CHUNK, D_MODEL), lambda i, c: (i, fout(c), 0)),
        out_shape=jax.ShapeDtypeStruct((b, s, D_MODEL), BF16),
        scratch_shapes=[state, pltpu.VMEM((SSD_CHUNK, D_MODEL), F32)],
        compiler_params=_params(2, 48),
    )(xs, bmat, cmat, dt, alog, expand_f, y_rev, z, dskip, gain)


ATT_ONES_ROWS = 16
ATT_DROP_NATS = 25.0


def _attn_kernel(tiles_ref, slopes_ref, q_ref, k_ref, v_ref, kt_ref, vt_ref,
                 lq1_ref, lk1_ref, lq2_ref, lk2_ref, sub_ref, o_ref,
                 vT_ref, vtT_ref, bias_ref, s0_ref, s1_ref, acc_ref, *, tq, tk, n_kt, n_qt, lam_init):
    h = pl.program_id(1)
    i = pl.program_id(2)
    lo = tiles_ref[3 * (h * n_qt + i)]
    hi = tiles_ref[3 * (h * n_qt + i) + 1]
    use_tail = tiles_ref[3 * (h * n_qt + i) + 2]
    slope = slopes_ref[h]
    vrows = ATT_V_DIM + ATT_ONES_ROWS

    @pl.when(i == 0)
    def _():
        for t in range(n_kt):
            vT_ref[t, 0:ATT_V_DIM, :] = v_ref[0, 0, t * tk:(t + 1) * tk, :].astype(F32).T.astype(BF16)
            vT_ref[t, ATT_V_DIM:vrows, :] = jnp.ones((ATT_ONES_ROWS, tk), BF16)
        vtT_ref[0:ATT_V_DIM, :] = vt_ref[...].astype(F32).T.astype(BF16)
        vtT_ref[ATT_V_DIM:vrows, :] = jnp.ones((ATT_ONES_ROWS, LANES), BF16)
        srel = slope * (lax.broadcasted_iota(jnp.int32, (tk, tq), 0)
                        - lax.broadcasted_iota(jnp.int32, (tk, tq), 1)).astype(F32)
        bias_ref[0] = -srel
        bias_ref[1] = srel
        bias_ref[2] = jnp.abs(srel)

    q = q_ref[0, 0]
    lane = lax.broadcasted_iota(jnp.int32, q.shape, 1)
    zero = jnp.zeros_like(q)
    qs = (jnp.where(lane < ATT_HALF_DIM, q, zero), jnp.where(lane >= ATT_HALF_DIM, q, zero))

    def scores(k, bias_tile):
        return [lax.dot_general(k, qs[e], (((1,), (1,)), ((), ())),
                                preferred_element_type=F32) - bias_tile for e in range(2)]

    def probs(s, m):
        return jnp.exp2(s - m).astype(BF16)

    q0 = (i * tq).astype(F32)
    n_tail = kt_ref.shape[0]

    def with_tail():
        tail_valid = lax.broadcasted_iota(jnp.int32, (n_tail, 1), 0) < N_META
        c_tail = slope * (q0 + N_META)
        ms = []
        for e, s in enumerate(scores(kt_ref[...], bias_ref[0, 0:n_tail, :])):
            s = jnp.where(tail_valid, s, -jnp.inf)
            m = jnp.max(s, axis=0, keepdims=True)
            acc_ref[e] = jnp.dot(vtT_ref[...], probs(s, m), preferred_element_type=F32)
            ms.append(m - c_tail)
        return tuple(ms)

    def without_tail():
        acc_ref[...] = jnp.zeros(acc_ref.shape, F32)
        return (jnp.full((1, tq), -jnp.inf, F32),) * 2

    ms = lax.cond(use_tail == 1, with_tail, without_tail)

    def stage_a(t, dst_ref):
        start = pl.multiple_of(t * tk, tk)
        side = jnp.where(t < i, 0, jnp.where(t > i, 1, 2))
        c = slope * jnp.abs(start.astype(F32) - q0)
        ss = scores(k_ref[0, 0, pl.ds(start, tk), :], bias_ref[side])
        for e in range(2):
            dst_ref[e] = ss[e]
        return tuple(jnp.max(ss[e], axis=0, keepdims=True) - c for e in range(2)) + (c,)

    def stage_b(t, src_ref, ms, cmax):
        vt_aug = vT_ref[t]
        out = []
        for e in range(2):
            m_new = jnp.maximum(ms[e], cmax[e])
            alpha = jnp.exp2(ms[e] - m_new)
            p = probs(src_ref[e], m_new + cmax[2])
            acc_ref[e] = alpha * acc_ref[e] + jnp.dot(vt_aug, p, preferred_element_type=F32)
            out.append(m_new)
        return tuple(out)

    def two_trips(u, carry):
        ms, cmax0 = carry
        t = lo + 2 * u
        cmax1 = stage_a(t + 1, s1_ref)
        ms = stage_b(t, s0_ref, ms, cmax0)
        cmax0 = stage_a(t + 2, s0_ref)
        ms = stage_b(t + 1, s1_ref, ms, cmax1)
        return ms, cmax0

    n_tiles = hi - lo
    n_two = (n_tiles - 1) // 2
    ms, cmax0 = lax.fori_loop(0, n_two, two_trips, (ms, stage_a(lo, s0_ref)))
    t_last = lo + 2 * n_two

    def finish_two(ms, cmax0):
        cmax1 = stage_a(t_last + 1, s1_ref)
        stage_b(t_last + 1, s1_ref, stage_b(t_last, s0_ref, ms, cmax0), cmax1)

    def finish_one(ms, cmax0):
        stage_b(t_last, s0_ref, ms, cmax0)

    lax.cond(n_tiles - 2 * n_two == 2, finish_two, finish_one, ms, cmax0)
    accs = (acc_ref[0], acc_ref[1])

    lam = (jnp.exp(jnp.sum(lq1_ref[...] * lk1_ref[...], axis=-1, keepdims=True))
           - jnp.exp(jnp.sum(lq2_ref[...] * lk2_ref[...], axis=-1, keepdims=True))
           + lam_init)
    o1, o2 = (a[0:ATT_V_DIM] / a[ATT_V_DIM:ATT_V_DIM + 1] for a in accs)
    o = o1 - lam * o2
    msq = jnp.mean(o * o, axis=0, keepdims=True)
    o = (o * lax.rsqrt(msq + EPS) * sub_ref[...]) * (1.0 - lam_init)
    o_ref[0] = o.T.astype(o_ref.dtype)


def _attention(q, k, v, k_tail, v_tail, q_gain, k_gain, lambdas, subln_g, layer_idx):
    b, _, s, _ = q.shape
    tq = tk = min(512, s)
    n_qt, n_kt = s // tq, s // tk
    lam_init = 0.8 - 0.6 * math.exp(-0.3 * layer_idx)
    slopes = jnp.exp2(-8.0 * jnp.arange(1, ATT_N_HEADS + 1, dtype=F32) / ATT_N_HEADS)
    bound = 1.01 * ATT_HALF_DIM ** 0.5 * jnp.max(jnp.abs(q_gain)) * jnp.max(jnp.abs(k_gain))
    reach = jnp.ceil((2.0 * bound + ATT_DROP_NATS) / slopes)
    reach = jnp.minimum(reach, float(2 * s)).astype(jnp.int32)[:, None]
    q_first = (jnp.arange(n_qt, dtype=jnp.int32) * tq)[None, :]
    lo = jnp.maximum(q_first - reach, 0) // tk
    hi = jnp.minimum((q_first + tq - 1 + reach) // tk + 1, n_kt)
    use_tail = (q_first + 1 <= reach).astype(jnp.int32)
    tiles = jnp.stack([lo, hi, use_tail], axis=-1).reshape(-1).astype(jnp.int32)
    kern = functools.partial(_attn_kernel, tq=tq, tk=tk, n_kt=n_kt, n_qt=n_qt, lam_init=lam_init)
    vec = lambda x: x.reshape(1, -1)
    small = lambda n: pl.BlockSpec((1, n), lambda bi, h, i: (0, 0))
    vrows = ATT_V_DIM + ATT_ONES_ROWS
    return pl.pallas_call(
        kern,
        grid=(b, ATT_N_HEADS, n_qt),
        in_specs=[
            pl.BlockSpec(memory_space=pltpu.SMEM),
            pl.BlockSpec(memory_space=pltpu.SMEM),
            pl.BlockSpec((1, 1, tq, LANES), lambda bi, h, i: (bi, h, i, 0)),
            pl.BlockSpec((1, 1, s, LANES), lambda bi, h, i: (bi, h, 0, 0)),
            pl.BlockSpec((1, 1, s, LANES), lambda bi, h, i: (bi, h, 0, 0)),
            pl.BlockSpec((LANES, LANES), lambda bi, h, i: (0, h)),
            pl.BlockSpec((LANES, LANES), lambda bi, h, i: (0, h)),
            small(ATT_HALF_DIM), small(ATT_HALF_DIM), small(ATT_HALF_DIM), small(ATT_HALF_DIM),
            pl.BlockSpec((ATT_V_DIM, 1), lambda bi, h, i: (0, 0)),
        ],
        out_specs=pl.BlockSpec((1, tq, LANES), lambda bi, h, i: (bi, i, h)),
        out_shape=jax.ShapeDtypeStruct((b, s, ATT_WIDTH), BF16),
        scratch_shapes=[
            pltpu.VMEM((n_kt, vrows, tk), BF16),
            pltpu.VMEM((vrows, LANES), BF16),
            pltpu.VMEM((3, tk, tq), F32),
            pltpu.VMEM((2, tk, tq), F32),
            pltpu.VMEM((2, tk, tq), F32),
            pltpu.VMEM((2, vrows, tq), F32),
        ],
        compiler_params=_params(3, 48),
    )(tiles, slopes * LOG2E, q, k, v, k_tail, v_tail, *[vec(x) for x in lambdas],
      subln_g.reshape(-1, 1))


def _proj(h_in, w, n, dtype, ep=None, **kw):
    return _matmul([h_in], [(w, 0)], [(0, 0)], ep or _ep_plain, n, dtype, 1024, min(1024, n), 52, **kw)


def _prepare(meta_tokens, wts):
    bf = lambda w: w.astype(BF16)
    o1 = D_MODEL
    o2 = o1 + SSD_CONV_CH
    o3 = o2 + 2 * SSD_N_HEADS
    o4 = o3 + ATT_WIDTH
    o5 = o4 + ATT_WIDTH
    w_in = wts["w_in"]
    p = dict(wts)
    p["w_z"], p["w_xbc"], p["w_dt"] = bf(w_in[:, :o1]), bf(w_in[:, o1:o2]), bf(w_in[:, o2:o3])
    p["w_q"], p["w_k"], p["w_v"] = bf(w_in[:, o3:o4]), bf(w_in[:, o4:o5]), bf(w_in[:, o5:])
    for name in ("w_branch_gate", "w_ssd_out", "w_att_out", "w_o", "w_down"):
        p[name] = bf(wts[name])
    p["w_gate_up"] = bf(wts["norm_ffn_g"][:, None] * wts["w_gate_up"])
    ones_bd = (jnp.arange(256)[:, None] // ATT_HALF_DIM
               == jnp.arange(256)[None, :] // ATT_HALF_DIM).astype(BF16)
    p["qk_norm"] = dict(
        q=dict(rows=[(jnp.tile(wts["q_norm_g"] * (ATT_HALF_DIM ** -0.5 * LOG2E),
                               2 * ATT_N_HEADS).reshape(1, ATT_WIDTH), 0)], consts=[ones_bd]),
        k=dict(rows=[(jnp.tile(wts["k_norm_g"], 2 * ATT_N_HEADS).reshape(1, ATT_WIDTH), 0)],
               consts=[ones_bd]))
    hn_meta = _rmsnorm(meta_tokens, wts["norm_mix_g"], N_META)
    tail_pad = ((0, LANES - N_META), (0, 0))
    p["xbc_meta"] = _proj(hn_meta, p["w_xbc"], SSD_CONV_CH, BF16)
    p["dt_raw_meta"] = _proj(hn_meta, p["w_dt"], 2 * SSD_N_HEADS, F32)
    p["k_tail"] = jnp.pad(_proj(hn_meta, p["w_k"], ATT_WIDTH, BF16, ep=_ep_qknorm,
                                **p["qk_norm"]["k"]), tail_pad)
    p["v_tail"] = jnp.pad(_proj(hn_meta, p["w_v"], ATT_WIDTH, BF16), tail_pad)
    return p


def _layer(x, layer_idx, p):
    b, s, d = x.shape
    m = b * s
    x2 = x.reshape(m, d)
    hn = _rmsnorm(x2, p["norm_mix_g"], 256)

    z = _proj(hn, p["w_z"], D_MODEL, BF16)
    xbc = _proj(hn, p["w_xbc"], SSD_CONV_CH, BF16)
    dt_raw = _proj(hn, p["w_dt"], 2 * SSD_N_HEADS, F32)
    q = _proj(hn, p["w_q"], ATT_WIDTH, BF16, ep=_ep_qknorm, seq_len=s, **p["qk_norm"]["q"])
    k = _proj(hn, p["w_k"], ATT_WIDTH, BF16, ep=_ep_qknorm, seq_len=s, **p["qk_norm"]["k"])
    v = _proj(hn, p["w_v"], ATT_WIDTH, BF16, seq_len=s)

    xs, bmat, cmat, dt = _conv_dt(
        xbc.reshape(b, s, SSD_CONV_CH), p["xbc_meta"], p["conv_w"], p["conv_b"],
        dt_raw.reshape(b, s, 2 * SSD_N_HEADS), p["dt_raw_meta"], p["dt_bias"].reshape(-1))
    y_ssd = _ssd(xs, bmat, cmat, dt, p["a_log"].reshape(-1), p["d_skip"],
                 p["ssd_norm_g"], z.reshape(b, s, D_MODEL))

    y_att = _attention(
        q, k, v, p["k_tail"], p["v_tail"], p["q_norm_g"], p["k_norm_g"],
        (p["lambda_q1"], p["lambda_k1"], p["lambda_q2"], p["lambda_k2"]),
        p["subln_g"], layer_idx)

    gates = _matmul([hn], [(p["w_branch_gate"], 0)], [(0, 0)], _ep_sigmoid_bias,
                    2 * D_MODEL, BF16, 1024, 1024, 52,
                    rows=[(p["b_branch_gate"].reshape(1, -1), 0)])
    bn_mix = 512
    mixed = _matmul([y_ssd.reshape(m, D_MODEL), y_att.reshape(m, ATT_WIDTH)],
                    [(p["w_ssd_out"], 0), (p["w_att_out"], 0)],
                    [(0, 0), (1, 1)], _ep_gated_merge, D_MODEL, BF16, 1024, bn_mix, 56,
                    tiles=[(gates, 0), (gates, D_MODEL // bn_mix)])
    h2, h2_bf, h2_sumsq = _matmul([mixed], [(p["w_o"], 0)], [(0, 0)], _ep_residual,
                                  D_MODEL, F32, 1024, 512, 52, tiles=[(x2, 0)], emit_stats=True)

    bn_ff = 256
    act = _matmul([h2_bf], [(p["w_gate_up"], 0), (p["w_gate_up"], D_FF // bn_ff)],
                  [(0, 0), (0, 1)], _ep_swiglu, D_FF, BF16, 2048, bn_ff, 52,
                  row_blocks=[h2_sumsq])
    out = _matmul([act], [(p["w_down"], 0)], [(0, 0)], _ep_residual,
                  D_MODEL, F32, 512, 512, 56, tiles=[(h2, 0)])
    return out.reshape(b, s, d)


def kernel(x_prompt, x_sample, meta_tokens, norm_mix_g, w_in, conv_w, conv_b, dt_bias, a_log, d_skip, ssd_norm_g, q_norm_g, k_norm_g, lambda_q1, lambda_k1, lambda_q2, lambda_k2, subln_g, w_branch_gate, b_branch_gate, w_ssd_out, w_att_out, w_o, norm_ffn_g, w_gate_up, w_down):
    names = ("norm_mix_g", "w_in", "conv_w", "conv_b", "dt_bias", "a_log", "d_skip",
             "ssd_norm_g", "q_norm_g", "k_norm_g", "lambda_q1", "lambda_k1", "lambda_q2",
             "lambda_k2", "subln_g", "w_branch_gate", "b_branch_gate", "w_ssd_out",
             "w_att_out", "w_o", "norm_ffn_g", "w_gate_up", "w_down")
    stacked = (norm_mix_g, w_in, conv_w, conv_b, dt_bias, a_log, d_skip, ssd_norm_g,
               q_norm_g, k_norm_g, lambda_q1, lambda_k1, lambda_q2, lambda_k2, subln_g,
               w_branch_gate, b_branch_gate, w_ssd_out, w_att_out, w_o, norm_ffn_g,
               w_gate_up, w_down)
    assert all(w.shape[0] == 1 for w in stacked), "single-layer encoder"
    p = _prepare(meta_tokens, {n: w[0] for n, w in zip(names, stacked)})
    return (_layer(x_prompt, 0, p), _layer(x_sample, 0, p))
```

```python
import functools
import math

import jax
import jax.numpy as jnp
from jax import lax
from jax.experimental import pallas as pl
from jax.experimental.pallas import tpu as pltpu

F32 = jnp.float32
BF16 = jnp.bfloat16

D_MODEL = 4096
N_META = 16
EPS = 1e-6
SSD_HEAD_DIM = 64
SSD_N_HEADS = D_MODEL // SSD_HEAD_DIM
SSD_N_GROUPS = 8
SSD_HEADS_PER_GROUP = SSD_N_HEADS // SSD_N_GROUPS
SSD_D_STATE = 128
SSD_CONV = 5
SSD_CHUNK = 128
SSD_GN = SSD_N_GROUPS * SSD_D_STATE
SSD_CONV_CH = D_MODEL + 2 * SSD_GN
SSD_GROUP_W = SSD_HEADS_PER_GROUP * SSD_HEAD_DIM
ATT_N_HEADS = 16
ATT_HALF_DIM = 64
ATT_V_DIM = 128
ATT_WIDTH = ATT_N_HEADS * ATT_V_DIM
D_FF = 11008
LANES = 128
HALO = 16
MIB = 1024 * 1024
LOG2E = 1.4426950408889634


def _params(n_axes, vmem_mib):
    return pltpu.CompilerParams(
        dimension_semantics=("arbitrary",) * n_axes,
        vmem_limit_bytes=vmem_mib * MIB)


def _rmsnorm_kernel(x_ref, g_ref, o_ref):
    x = x_ref[...]
    ms = jnp.mean(x * x, axis=-1, keepdims=True)
    o_ref[...] = (x * lax.rsqrt(ms + EPS) * g_ref[...]).astype(o_ref.dtype)


def _rmsnorm(x, g, bm):
    m, d = x.shape
    bm = min(bm, m)
    return pl.pallas_call(
        _rmsnorm_kernel,
        grid=(m // bm,),
        in_specs=[pl.BlockSpec((bm, d), lambda i: (i, 0)),
                  pl.BlockSpec((1, d), lambda i: (0, 0))],
        out_specs=pl.BlockSpec((bm, d), lambda i: (i, 0)),
        out_shape=jax.ShapeDtypeStruct((m, d), BF16),
        compiler_params=_params(1, 32),
    )(x, g.reshape(1, d))


def _mm_kernel(*refs, dots, n_lhs, n_rhs, n_tiles, n_rows, n_consts, epilogue, emit_stats,
               head_major):
    lhs = refs[:n_lhs]
    rhs = refs[n_lhs:n_lhs + n_rhs]
    p = n_lhs + n_rhs
    tiles = refs[p:p + n_tiles]
    p += n_tiles
    rows = refs[p:p + n_rows]
    p += n_rows
    consts = refs[p:p + n_consts]
    p += n_consts
    o_ref = refs[p]
    accs = [jnp.dot(lhs[a][...], rhs[b][...], preferred_element_type=F32)
            for a, b in dots]
    out = epilogue(accs, [t[...] for t in tiles], [r[...] for r in rows],
                   [c[...] for c in consts])
    if head_major:
        for hh in range(o_ref.shape[1]):
            o_ref[0, hh] = out[:, hh * LANES:(hh + 1) * LANES].astype(o_ref.dtype)
    else:
        o_ref[...] = out.astype(o_ref.dtype)
    if emit_stats:
        obf_ref, ss_ref = refs[p + 1], refs[p + 2]
        obf_ref[...] = out.astype(obf_ref.dtype)

        @pl.when(pl.program_id(1) == 0)
        def _():
            ss_ref[...] = jnp.zeros(ss_ref.shape, F32)

        ss_ref[...] += jnp.broadcast_to(jnp.sum(out * out, axis=1, keepdims=True), ss_ref.shape)


def _matmul(lhs, rhs, dots, epilogue, n_out, out_dtype, bm, bn, vmem_mib,
            tiles=(), rows=(), consts=(), row_blocks=(), emit_stats=False, seq_len=None):
    m = lhs[0].shape[0]
    bm = min(bm, m)
    in_specs = []
    args = []
    for a in lhs:
        in_specs.append(pl.BlockSpec((bm, a.shape[1]), lambda i, j: (i, 0)))
        args.append(a)
    for w, off in rhs:
        in_specs.append(pl.BlockSpec((w.shape[0], bn), lambda i, j, off=off: (0, j + off)))
        args.append(w)
    for t, off in tiles:
        in_specs.append(pl.BlockSpec((bm, bn), lambda i, j, off=off: (i, j + off)))
        args.append(t)
    for t in row_blocks:
        in_specs.append(pl.BlockSpec((bm, t.shape[1]), lambda i, j: (i, 0)))
        args.append(t)
    for r, off in rows:
        in_specs.append(pl.BlockSpec((1, bn), lambda i, j, off=off: (0, j + off)))
        args.append(r)
    for c in consts:
        in_specs.append(pl.BlockSpec(c.shape, lambda i, j, nd=c.ndim: (0,) * nd))
        args.append(c)
    out_specs = pl.BlockSpec((bm, bn), lambda i, j: (i, j))
    out_shape = jax.ShapeDtypeStruct((m, n_out), out_dtype)
    if seq_len is not None:
        assert seq_len % bm == 0 and bn % LANES == 0
        per_seq = seq_len // bm
        out_specs = pl.BlockSpec((1, bn // LANES, bm, LANES),
                                 lambda i, j: (i // per_seq, j, i % per_seq, 0))
        out_shape = jax.ShapeDtypeStruct((m // seq_len, n_out // LANES, seq_len, LANES), out_dtype)
    if emit_stats:
        out_specs = [out_specs, pl.BlockSpec((bm, bn), lambda i, j: (i, j)),
                     pl.BlockSpec((bm, LANES), lambda i, j: (i, 0))]
        out_shape = [out_shape, jax.ShapeDtypeStruct((m, n_out), BF16),
                     jax.ShapeDtypeStruct((m, LANES), F32)]
    kern = functools.partial(
        _mm_kernel, dots=tuple(dots), n_lhs=len(lhs), n_rhs=len(rhs),
        n_tiles=len(tiles) + len(row_blocks), n_rows=len(rows), n_consts=len(consts),
        epilogue=epilogue, emit_stats=emit_stats, head_major=seq_len is not None)
    return pl.pallas_call(
        kern,
        grid=(m // bm, n_out // bn),
        in_specs=in_specs,
        out_specs=out_specs,
        out_shape=out_shape,
        compiler_params=_params(2, vmem_mib),
    )(*args)


def _ep_plain(accs, tiles, rows, consts):
    return accs[0]


def _ep_qknorm(accs, tiles, rows, consts):
    acc = accs[0]
    ones_bd = consts[0]
    width = ones_bd.shape[0]
    outs = []
    for c in range(acc.shape[1] // width):
        a = acc[:, c * width:(c + 1) * width]
        ss = jnp.dot((a * a).astype(BF16), ones_bd, preferred_element_type=F32)
        outs.append(a * lax.rsqrt(ss * (1.0 / ATT_HALF_DIM) + EPS))
    return jnp.concatenate(outs, axis=1) * rows[0]


def _ep_sigmoid_bias(accs, tiles, rows, consts):
    return 0.5 * jnp.tanh(0.5 * (accs[0] + rows[0])) + 0.5


def _ep_gated_merge(accs, tiles, rows, consts):
    return tiles[0].astype(F32) * accs[0] + tiles[1].astype(F32) * accs[1]


def _ep_residual(accs, tiles, rows, consts):
    return tiles[0] + accs[0]


def _ep_swiglu(accs, tiles, rows, consts):
    inv_rms = lax.rsqrt(tiles[0][:, 0:1] * (1.0 / D_MODEL) + EPS)
    g = accs[0] * inv_rms
    half_g = 0.5 * g
    return (half_g * jnp.tanh(half_g) + half_g) * (accs[1] * inv_rms)


def _conv_kernel(cur_ref, prev_ref, next_ref, meta_ref, w_ref, b_ref, shift_ref,
                 dtr_ref, dtm_ref, dtb_ref,
                 xs_ref, bm_ref, cm_ref, dt_ref, ext_ref, *, n_chunks):
    c = pl.program_id(1)
    pad = SSD_CHUNK - N_META
    ch = ext_ref.shape[1]
    dtype = ext_ref.dtype

    @pl.when(c == 0)
    def _():
        ext_ref[0:HALO + pad, :] = jnp.zeros((HALO + pad, ch), dtype)
        ext_ref[HALO + pad:HALO + SSD_CHUNK, :] = meta_ref[...]

    @pl.when(c == 1)
    def _():
        ext_ref[0:HALO, :] = meta_ref[...]

    @pl.when(c > 1)
    def _():
        ext_ref[0:HALO, :] = prev_ref[0]

    @pl.when(c > 0)
    def _():
        ext_ref[HALO:HALO + SSD_CHUNK, :] = cur_ref[0]

    @pl.when(c < n_chunks - 1)
    def _():
        ext_ref[HALO + SSD_CHUNK:, :] = next_ref[0]

    @pl.when(c == n_chunks - 1)
    def _():
        ext_ref[HALO + SSD_CHUNK:, :] = jnp.zeros((HALO, ch), dtype)

    half = (SSD_CONV - 1) // 2

    def conv_silu(lo, hi):
        ext = ext_ref[:, lo:hi]
        shifted = jnp.dot(shift_ref[...], ext, preferred_element_type=F32)
        acc = None
        idx = 0
        for k in range(SSD_CONV):
            if k == half:
                x = ext[HALO:HALO + SSD_CHUNK].astype(F32)
            else:
                x = shifted[idx * SSD_CHUNK:(idx + 1) * SSD_CHUNK]
                idx += 1
            term = x * w_ref[k:k + 1, lo:hi]
            acc = term if acc is None else acc + term
        acc = acc + b_ref[:, lo:hi]
        half_acc = 0.5 * acc
        return half_acc * jnp.tanh(half_acc) + half_acc

    blk = 1024
    for s in range(D_MODEL // blk):
        xs_ref[0, :, s * blk:(s + 1) * blk] = conv_silu(s * blk, (s + 1) * blk).astype(xs_ref.dtype)
    bm_ref[0] = conv_silu(D_MODEL, D_MODEL + SSD_GN).astype(bm_ref.dtype)
    cm_ref[0] = conv_silu(D_MODEL + SSD_GN, SSD_CONV_CH).astype(cm_ref.dtype)

    @pl.when(c == 0)
    def _():
        xs_ref[0, 0:pad, :] = jnp.zeros((pad, D_MODEL), xs_ref.dtype)
        bm_ref[0, 0:pad, :] = jnp.zeros((pad, SSD_GN), bm_ref.dtype)
        cm_ref[0, 0:pad, :] = jnp.zeros((pad, SSD_GN), cm_ref.dtype)

    row = lax.broadcasted_iota(jnp.int32, (SSD_CHUNK, dt_ref.shape[2]), 0)
    valid = jnp.logical_or(c > 0, row >= pad)

    raw_real = dtr_ref[0]
    raw_meta = jnp.concatenate(
        [jnp.zeros((pad, raw_real.shape[1]), F32), dtm_ref[...]], axis=0)
    raw = jnp.where(c > 0, raw_real, raw_meta) + dtb_ref[...]
    sp = jnp.maximum(raw, 0.0) + jnp.log1p(jnp.exp(-jnp.abs(raw)))
    dt_ref[0] = jnp.where(valid, sp, 0.0)


def _conv_dt(xbc, xbc_meta, conv_w, conv_b, dt_raw, dt_raw_meta, dt_bias):
    b, s, ch = xbc.shape
    n_chunks = s // SSD_CHUNK + 1
    lp = n_chunks * SSD_CHUNK
    hb = SSD_CHUNK // HALO
    n_halo = s // HALO
    nd = dt_raw.shape[-1]
    kern = functools.partial(_conv_kernel, n_chunks=n_chunks)
    n_ext = 2 * HALO + SSD_CHUNK
    half = (SSD_CONV - 1) // 2
    taps = jnp.array([k - half for k in range(SSD_CONV) if k != half], jnp.int32)
    src = HALO + jnp.arange(SSD_CHUNK, dtype=jnp.int32)[None, :] + taps[:, None]
    shift = (src.reshape(-1, 1) == jnp.arange(n_ext, dtype=jnp.int32)[None, :]).astype(BF16)
    return pl.pallas_call(
        kern,
        grid=(b, n_chunks),
        in_specs=[
            pl.BlockSpec((1, SSD_CHUNK, ch), lambda i, c: (i, jnp.maximum(c - 1, 0), 0)),
            pl.BlockSpec((1, HALO, ch), lambda i, c: (i, jnp.maximum((c - 1) * hb - 1, 0), 0)),
            pl.BlockSpec((1, HALO, ch), lambda i, c: (i, jnp.minimum(c * hb, n_halo - 1), 0)),
            pl.BlockSpec((N_META, ch), lambda i, c: (0, 0)),
            pl.BlockSpec((SSD_CONV, ch), lambda i, c: (0, 0)),
            pl.BlockSpec((1, ch), lambda i, c: (0, 0)),
            pl.BlockSpec(((SSD_CONV - 1) * SSD_CHUNK, n_ext), lambda i, c: (0, 0)),
            pl.BlockSpec((1, SSD_CHUNK, nd), lambda i, c: (i, jnp.maximum(c - 1, 0), 0)),
            pl.BlockSpec((N_META, nd), lambda i, c: (0, 0)),
            pl.BlockSpec((1, nd), lambda i, c: (0, 0)),
        ],
        out_specs=[
            pl.BlockSpec((1, SSD_CHUNK, D_MODEL), lambda i, c: (i, c, 0)),
            pl.BlockSpec((1, SSD_CHUNK, SSD_GN), lambda i, c: (i, c, 0)),
            pl.BlockSpec((1, SSD_CHUNK, SSD_GN), lambda i, c: (i, c, 0)),
            pl.BlockSpec((1, SSD_CHUNK, nd), lambda i, c: (i, c, 0)),
        ],
        out_shape=[
            jax.ShapeDtypeStruct((b, lp, D_MODEL), BF16),
            jax.ShapeDtypeStruct((b, lp, SSD_GN), BF16),
            jax.ShapeDtypeStruct((b, lp, SSD_GN), BF16),
            jax.ShapeDtypeStruct((b, lp, nd), F32),
        ],
        scratch_shapes=[pltpu.VMEM((n_ext, ch), BF16)],
        compiler_params=_params(2, 40),
    )(xbc, xbc, xbc, xbc_meta, conv_w, conv_b.reshape(1, ch), shift,
      dt_raw, dt_raw_meta, dt_bias.reshape(1, nd))


def _split3(x):
    hi = x.astype(BF16)
    r = x - hi.astype(F32)
    mid = r.astype(BF16)
    lo = (r - mid.astype(F32)).astype(BF16)
    return hi, mid, lo


def _ssd_body(x_ref, b_ref, c_ref, dt_ref, alog_ref, exp_ref, state_ref, *, reverse):
    t = SSD_CHUNK
    lane0 = SSD_N_HEADS if reverse else 0
    dt = dt_ref[0]
    a = dt * (-jnp.exp(alog_ref[...]) * LOG2E)
    ri = lax.broadcasted_iota(jnp.int32, (t, t), 0)
    ci = lax.broadcasted_iota(jnp.int32, (t, t), 1)
    causal = (ci >= ri) if reverse else (ci <= ri)
    tri = causal.astype(BF16)
    a3 = _split3(a)
    cs = sum(jnp.dot(tri, p, preferred_element_type=F32) for p in a3)
    cs_t = cs.T
    dt_t = dt.T
    edge = 0 if reverse else t - 1
    cs_edge = cs[edge:edge + 1, :]
    state_in = dt * jnp.exp2(cs_edge - cs)
    edge_parts = jnp.concatenate(
        [p.astype(F32) for p in _split3(jnp.exp2(cs_edge))]
        + [jnp.zeros((HALO - 3, LANES), F32)], axis=0)
    stacked = jnp.concatenate([state_in, edge_parts], axis=0).astype(BF16)
    expanded = jnp.dot(stacked, exp_ref[...], preferred_element_type=F32)
    chunk_decay_all = expanded[t:t + 1] + expanded[t + 1:t + 2] + expanded[t + 2:t + 3]
    first_half = lax.broadcasted_iota(jnp.int32, (t, LANES), 1) < SSD_HEAD_DIM
    outs = []
    for g in range(SSD_N_GROUPS):
        gs = slice(g * SSD_GROUP_W, (g + 1) * SSD_GROUP_W)
        ns = slice(g * SSD_D_STATE, (g + 1) * SSD_D_STATE)
        bg = b_ref[0, :, ns]
        cg = c_ref[0, :, ns]
        cb = lax.dot_general(cg, bg, (((1,), (1,)), ((), ())), preferred_element_type=F32)
        cbm = jnp.where(causal, cb, 0.0)
        xg = x_ref[0, :, gs]
        x_in = (xg.astype(F32) * expanded[0:t, gs]).astype(BF16)
        new_state = lax.dot_general(bg, x_in, (((0,), (0,)), ((), ())),
                                    preferred_element_type=F32)
        prev = state_ref[g]
        y_off = jnp.dot(cg, prev.astype(BF16), preferred_element_type=F32)
        state_ref[g] = prev * chunk_decay_all[:, gs] + new_state
        for jp in range(SSD_HEADS_PER_GROUP // 2):
            xp = xg[:, jp * LANES:(jp + 1) * LANES]
            zero = jnp.zeros_like(xp)
            rhs = jnp.concatenate([jnp.where(first_half, xp, zero),
                                   jnp.where(first_half, zero, xp)], axis=0)
            mats, ecols = [], []
            for e in range(2):
                hl = lane0 + g * SSD_HEADS_PER_GROUP + 2 * jp + e
                colb = jnp.broadcast_to(cs[:, hl:hl + 1], (t, t))
                seg = jnp.minimum(colb - cs_t[hl:hl + 1, :], 0.0)
                mats.append((cbm * jnp.exp2(seg) * dt_t[hl:hl + 1, :]).astype(BF16))
                ecols.append(jnp.exp2(colb))
            y_diag = jnp.dot(jnp.concatenate(mats, axis=1), rhs, preferred_element_type=F32)
            y_pair = y_diag + (y_off[:, jp * LANES:(jp + 1) * LANES]
                               * jnp.where(first_half, ecols[0], ecols[1]))
            outs.append(y_pair)
    return outs


def _ssd_rev_kernel(x_ref, b_ref, c_ref, dt_ref, alog_ref, exp_ref, y_ref, state_ref, *, n_chunks):
    step = pl.program_id(1)

    @pl.when(step == 0)
    def _():
        state_ref[...] = jnp.zeros(state_ref.shape, F32)

    outs = _ssd_body(x_ref, b_ref, c_ref, dt_ref, alog_ref, exp_ref, state_ref, reverse=True)

    @pl.when(step < n_chunks - 1)
    def _():
        for k, y in enumerate(outs):
            y_ref[0, :, k * LANES:(k + 1) * LANES] = y


def _ssd_fwd_kernel(x_ref, b_ref, c_ref, dt_ref, alog_ref, exp_ref, yb_ref, z_ref,
                    dskip_ref, gain_ref, o_ref, state_ref, ybuf_ref):
    step = pl.program_id(1)

    @pl.when(step == 0)
    def _():
        state_ref[...] = jnp.zeros(state_ref.shape, F32)

    outs = _ssd_body(x_ref, b_ref, c_ref, dt_ref, alog_ref, exp_ref, state_ref, reverse=False)

    @pl.when(step > 0)
    def _():
        for k, y in enumerate(outs):
            ls = slice(k * LANES, (k + 1) * LANES)
            x = x_ref[0, :, ls].astype(F32)
            z = z_ref[0, :, ls].astype(F32)
            yk = (y + yb_ref[0, :, ls]) + x * dskip_ref[:, ls]
            half_z = 0.5 * z
            ybuf_ref[:, ls] = yk * (half_z * jnp.tanh(half_z) + half_z)
        for g in range(SSD_N_GROUPS):
            gs = slice(g * SSD_GROUP_W, (g + 1) * SSD_GROUP_W)
            yg = ybuf_ref[:, gs]
            ms = jnp.mean(yg * yg, axis=-1, keepdims=True)
            o_ref[0, :, gs] = (yg * lax.rsqrt(ms + EPS) * gain_ref[:, gs]).astype(o_ref.dtype)


def _ssd(xs, bmat, cmat, dt, a_log, d_skip, ssd_norm_g, z):
    b, lp, _ = xs.shape
    n_chunks = lp // SSD_CHUNK
    s = lp - SSD_CHUNK
    nd = dt.shape[-1]
    alog = a_log.reshape(1, nd)
    heads = jnp.arange(D_MODEL, dtype=jnp.int32) // SSD_HEAD_DIM
    lanes = jnp.arange(nd, dtype=jnp.int32)
    expand_f = (lanes[:, None] == heads[None, :]).astype(BF16)
    expand_r = (lanes[:, None] == heads[None, :] + SSD_N_HEADS).astype(BF16)
    dskip = jnp.repeat(d_skip, SSD_HEAD_DIM).reshape(1, D_MODEL)
    gain = ssd_norm_g.reshape(1, D_MODEL)
    state = pltpu.VMEM((SSD_N_GROUPS, SSD_D_STATE, SSD_GROUP_W), F32)

    def chunk_specs(cmap):
        return [
            pl.BlockSpec((1, SSD_CHUNK, D_MODEL), lambda i, c: (i, cmap(c), 0)),
            pl.BlockSpec((1, SSD_CHUNK, SSD_GN), lambda i, c: (i, cmap(c), 0)),
            pl.BlockSpec((1, SSD_CHUNK, SSD_GN), lambda i, c: (i, cmap(c), 0)),
            pl.BlockSpec((1, SSD_CHUNK, nd), lambda i, c: (i, cmap(c), 0)),
            pl.BlockSpec((1, nd), lambda i, c: (0, 0)),
            pl.BlockSpec((nd, D_MODEL), lambda i, c: (0, 0)),
        ]

    rmap = lambda c: n_chunks - 1 - c
    rout = lambda c: jnp.maximum(n_chunks - 2 - c, 0)
    y_rev = pl.pallas_call(
        functools.partial(_ssd_rev_kernel, n_chunks=n_chunks),
        grid=(b, n_chunks),
        in_specs=chunk_specs(rmap),
        out_specs=pl.BlockSpec((1, SSD_CHUNK, D_MODEL), lambda i, c: (i, rout(c), 0)),
        out_shape=jax.ShapeDtypeStruct((b, s, D_MODEL), F32),
        scratch_shapes=[state],
        compiler_params=_params(2, 48),
    )(xs, bmat, cmat, dt, alog, expand_r)

    fmap = lambda c: c
    fout = lambda c: jnp.maximum(c - 1, 0)
    return pl.pallas_call(
        _ssd_fwd_kernel,
        grid=(b, n_chunks),
        in_specs=chunk_specs(fmap) + [
            pl.BlockSpec((1, SSD_CHUNK, D_MODEL), lambda i, c: (i, fout(c), 0)),
            pl.BlockSpec((1, SSD_CHUNK, D_MODEL), lambda i, c: (i, fout(c), 0)),
            pl.BlockSpec((1, D_MODEL), lambda i, c: (0, 0)),
            pl.BlockSpec((1, D_MODEL), lambda i, c: (0, 0)),
        ],
        out_specs=pl.BlockSpec((1, SSD_CHUNK, D_MODEL), lambda i, c: (i, fout(c), 0)),
        out_shape=jax.ShapeDtypeStruct((b, s, D_MODEL), BF16),
        scratch_shapes=[state, pltpu.VMEM((SSD_CHUNK, D_MODEL), F32)],
        compiler_params=_params(2, 48),
    )(xs, bmat, cmat, dt, alog, expand_f, y_rev, z, dskip, gain)


ATT_ONES_ROWS = 16
ATT_DROP_NATS = 25.0


def _attn_kernel(tiles_ref, slopes_ref, q_ref, k_ref, v_ref, kt_ref, vt_ref,
                 lq1_ref, lk1_ref, lq2_ref, lk2_ref, sub_ref, o_ref,
                 vT_ref, vtT_ref, bias_ref, s0_ref, s1_ref, acc_ref, *, tq, tk, n_kt, n_qt, lam_init):
    h = pl.program_id(1)
    i = pl.program_id(2)
    lo = tiles_ref[3 * (h * n_qt + i)]
    hi = tiles_ref[3 * (h * n_qt + i) + 1]
    use_tail = tiles_ref[3 * (h * n_qt + i) + 2]
    slope = slopes_ref[h]
    vrows = ATT_V_DIM + ATT_ONES_ROWS

    @pl.when(i == 0)
    def _():
        for t in range(n_kt):
            vT_ref[t, 0:ATT_V_DIM, :] = v_ref[0, 0, t * tk:(t + 1) * tk, :].astype(F32).T.astype(BF16)
            vT_ref[t, ATT_V_DIM:vrows, :] = jnp.ones((ATT_ONES_ROWS, tk), BF16)
        vtT_ref[0:ATT_V_DIM, :] = vt_ref[...].astype(F32).T.astype(BF16)
        vtT_ref[ATT_V_DIM:vrows, :] = jnp.ones((ATT_ONES_ROWS, LANES), BF16)
        srel = slope * (lax.broadcasted_iota(jnp.int32, (tk, tq), 0)
                        - lax.broadcasted_iota(jnp.int32, (tk, tq), 1)).astype(F32)
        bias_ref[0] = -srel
        bias_ref[1] = srel
        bias_ref[2] = jnp.abs(srel)

    q = q_ref[0, 0]
    lane = lax.broadcasted_iota(jnp.int32, q.shape, 1)
    zero = jnp.zeros_like(q)
    qs = (jnp.where(lane < ATT_HALF_DIM, q, zero), jnp.where(lane >= ATT_HALF_DIM, q, zero))

    def scores(k, bias_tile):
        return [lax.dot_general(k, qs[e], (((1,), (1,)), ((), ())),
                                preferred_element_type=F32) - bias_tile for e in range(2)]

    def probs(s, m):
        return jnp.exp2(s - m).astype(BF16)

    q0 = (i * tq).astype(F32)
    n_tail = kt_ref.shape[0]

    def with_tail():
        tail_valid = lax.broadcasted_iota(jnp.int32, (n_tail, 1), 0) < N_META
        c_tail = slope * (q0 + N_META)
        ms = []
        for e, s in enumerate(scores(kt_ref[...], bias_ref[0, 0:n_tail, :])):
            s = jnp.where(tail_valid, s, -jnp.inf)
            m = jnp.max(s, axis=0, keepdims=True)
            acc_ref[e] = jnp.dot(vtT_ref[...], probs(s, m), preferred_element_type=F32)
            ms.append(m - c_tail)
        return tuple(ms)

    def without_tail():
        acc_ref[...] = jnp.zeros(acc_ref.shape, F32)
        return (jnp.full((1, tq), -jnp.inf, F32),) * 2

    ms = lax.cond(use_tail == 1, with_tail, without_tail)

    n_half = tq // 256

    def stage_a(t, dst_ref):
        start = pl.multiple_of(t * tk, tk)
        side = jnp.where(t < i, 0, jnp.where(t > i, 1, 2))
        c = slope * jnp.abs(start.astype(F32) - q0)
        k = k_ref[0, 0, pl.ds(start, tk), :]
        cmax = [[], []]
        for hq in range(n_half):
            qsl = slice(hq * 256, (hq + 1) * 256)
            bias_tile = bias_ref[side, :, qsl]
            for e in range(2):
                s = lax.dot_general(k, qs[e][qsl], (((1,), (1,)), ((), ())),
                                    preferred_element_type=F32) - bias_tile
                dst_ref[e, :, qsl] = s
                cmax[e].append(jnp.max(s, axis=0, keepdims=True))
        return tuple(jnp.concatenate(cmax[e], axis=1) - c for e in range(2)) + (c,)

    def stage_b(t, src_ref, ms, cmax):
        vt_aug = vT_ref[t]
        out = []
        for e in range(2):
            m_new = jnp.maximum(ms[e], cmax[e])
            alpha = jnp.exp2(ms[e] - m_new)
            shift = m_new + cmax[2]
            for hq in range(n_half):
                qsl = slice(hq * 256, (hq + 1) * 256)
                p = probs(src_ref[e, :, qsl], shift[:, qsl])
                acc_ref[e, :, qsl] = (alpha[:, qsl] * acc_ref[e, :, qsl]
                                      + jnp.dot(vt_aug, p, preferred_element_type=F32))
            out.append(m_new)
        return tuple(out)

    def two_trips(u, carry):
        ms, cmax0 = carry
        t = lo + 2 * u
        cmax1 = stage_a(t + 1, s1_ref)
        ms = stage_b(t, s0_ref, ms, cmax0)
        cmax0 = stage_a(t + 2, s0_ref)
        ms = stage_b(t + 1, s1_ref, ms, cmax1)
        return ms, cmax0

    n_tiles = hi - lo
    n_two = (n_tiles - 1) // 2
    ms, cmax0 = lax.fori_loop(0, n_two, two_trips, (ms, stage_a(lo, s0_ref)))
    t_last = lo + 2 * n_two

    def finish_two(ms, cmax0):
        cmax1 = stage_a(t_last + 1, s1_ref)
        stage_b(t_last + 1, s1_ref, stage_b(t_last, s0_ref, ms, cmax0), cmax1)

    def finish_one(ms, cmax0):
        stage_b(t_last, s0_ref, ms, cmax0)

    lax.cond(n_tiles - 2 * n_two == 2, finish_two, finish_one, ms, cmax0)
    accs = (acc_ref[0], acc_ref[1])

    lam = (jnp.exp(jnp.sum(lq1_ref[...] * lk1_ref[...], axis=-1, keepdims=True))
           - jnp.exp(jnp.sum(lq2_ref[...] * lk2_ref[...], axis=-1, keepdims=True))
           + lam_init)
    o1, o2 = (a[0:ATT_V_DIM] / a[ATT_V_DIM:ATT_V_DIM + 1] for a in accs)
    o = o1 - lam * o2
    msq = jnp.mean(o * o, axis=0, keepdims=True)
    o = (o * lax.rsqrt(msq + EPS) * sub_ref[...]) * (1.0 - lam_init)
    o_ref[0] = o.T.astype(o_ref.dtype)


def _attention(q, k, v, k_tail, v_tail, q_gain, k_gain, lambdas, subln_g, layer_idx):
    b, _, s, _ = q.shape
    tq = tk = min(512, s)
    n_qt, n_kt = s // tq, s // tk
    lam_init = 0.8 - 0.6 * math.exp(-0.3 * layer_idx)
    slopes = jnp.exp2(-8.0 * jnp.arange(1, ATT_N_HEADS + 1, dtype=F32) / ATT_N_HEADS)
    bound = 1.01 * ATT_HALF_DIM ** 0.5 * jnp.max(jnp.abs(q_gain)) * jnp.max(jnp.abs(k_gain))
    reach = jnp.ceil((2.0 * bound + ATT_DROP_NATS) / slopes)
    reach = jnp.minimum(reach, float(2 * s)).astype(jnp.int32)[:, None]
    q_first = (jnp.arange(n_qt, dtype=jnp.int32) * tq)[None, :]
    lo = jnp.maximum(q_first - reach, 0) // tk
    hi = jnp.minimum((q_first + tq - 1 + reach) // tk + 1, n_kt)
    use_tail = (q_first + 1 <= reach).astype(jnp.int32)
    tiles = jnp.stack([lo, hi, use_tail], axis=-1).reshape(-1).astype(jnp.int32)
    kern = functools.partial(_attn_kernel, tq=tq, tk=tk, n_kt=n_kt, n_qt=n_qt, lam_init=lam_init)
    vec = lambda x: x.reshape(1, -1)
    small = lambda n: pl.BlockSpec((1, n), lambda bi, h, i: (0, 0))
    vrows = ATT_V_DIM + ATT_ONES_ROWS
    return pl.pallas_call(
        kern,
        grid=(b, ATT_N_HEADS, n_qt),
        in_specs=[
            pl.BlockSpec(memory_space=pltpu.SMEM),
            pl.BlockSpec(memory_space=pltpu.SMEM),
            pl.BlockSpec((1, 1, tq, LANES), lambda bi, h, i: (bi, h, i, 0)),
            pl.BlockSpec((1, 1, s, LANES), lambda bi, h, i: (bi, h, 0, 0)),
            pl.BlockSpec((1, 1, s, LANES), lambda bi, h, i: (bi, h, 0, 0)),
            pl.BlockSpec((LANES, LANES), lambda bi, h, i: (0, h)),
            pl.BlockSpec((LANES, LANES), lambda bi, h, i: (0, h)),
            small(ATT_HALF_DIM), small(ATT_HALF_DIM), small(ATT_HALF_DIM), small(ATT_HALF_DIM),
            pl.BlockSpec((ATT_V_DIM, 1), lambda bi, h, i: (0, 0)),
        ],
        out_specs=pl.BlockSpec((1, tq, LANES), lambda bi, h, i: (bi, i, h)),
        out_shape=jax.ShapeDtypeStruct((b, s, ATT_WIDTH), BF16),
        scratch_shapes=[
            pltpu.VMEM((n_kt, vrows, tk), BF16),
            pltpu.VMEM((vrows, LANES), BF16),
            pltpu.VMEM((3, tk, tq), F32),
            pltpu.VMEM((2, tk, tq), F32),
            pltpu.VMEM((2, tk, tq), F32),
            pltpu.VMEM((2, vrows, tq), F32),
        ],
        compiler_params=_params(3, 48),
    )(tiles, slopes * LOG2E, q, k, v, k_tail, v_tail, *[vec(x) for x in lambdas],
      subln_g.reshape(-1, 1))


def _proj(h_in, w, n, dtype, ep=None, **kw):
    return _matmul([h_in], [(w, 0)], [(0, 0)], ep or _ep_plain, n, dtype, 1024, min(1024, n), 52, **kw)


def _prepare(meta_tokens, wts):
    bf = lambda w: w.astype(BF16)
    o1 = D_MODEL
    o2 = o1 + SSD_CONV_CH
    o3 = o2 + 2 * SSD_N_HEADS
    o4 = o3 + ATT_WIDTH
    o5 = o4 + ATT_WIDTH
    w_in = wts["w_in"]
    p = dict(wts)
    p["w_z"], p["w_xbc"], p["w_dt"] = bf(w_in[:, :o1]), bf(w_in[:, o1:o2]), bf(w_in[:, o2:o3])
    p["w_q"], p["w_k"], p["w_v"] = bf(w_in[:, o3:o4]), bf(w_in[:, o4:o5]), bf(w_in[:, o5:])
    for name in ("w_branch_gate", "w_ssd_out", "w_att_out", "w_o", "w_down"):
        p[name] = bf(wts[name])
    p["w_gate_up"] = bf(wts["norm_ffn_g"][:, None] * wts["w_gate_up"])
    ones_bd = (jnp.arange(256)[:, None] // ATT_HALF_DIM
               == jnp.arange(256)[None, :] // ATT_HALF_DIM).astype(BF16)
    p["qk_norm"] = dict(
        q=dict(rows=[(jnp.tile(wts["q_norm_g"] * (ATT_HALF_DIM ** -0.5 * LOG2E),
                               2 * ATT_N_HEADS).reshape(1, ATT_WIDTH), 0)], consts=[ones_bd]),
        k=dict(rows=[(jnp.tile(wts["k_norm_g"], 2 * ATT_N_HEADS).reshape(1, ATT_WIDTH), 0)],
               consts=[ones_bd]))
    hn_meta = _rmsnorm(meta_tokens, wts["norm_mix_g"], N_META)
    tail_pad = ((0, LANES - N_META), (0, 0))
    p["xbc_meta"] = _proj(hn_meta, p["w_xbc"], SSD_CONV_CH, BF16)
    p["dt_raw_meta"] = _proj(hn_meta, p["w_dt"], 2 * SSD_N_HEADS, F32)
    p["k_tail"] = jnp.pad(_proj(hn_meta, p["w_k"], ATT_WIDTH, BF16, ep=_ep_qknorm,
                                **p["qk_norm"]["k"]), tail_pad)
    p["v_tail"] = jnp.pad(_proj(hn_meta, p["w_v"], ATT_WIDTH, BF16), tail_pad)
    return p


def _layer(x, layer_idx, p):
    b, s, d = x.shape
    m = b * s
    x2 = x.reshape(m, d)
    hn = _rmsnorm(x2, p["norm_mix_g"], 256)

    z = _proj(hn, p["w_z"], D_MODEL, BF16)
    xbc = _proj(hn, p["w_xbc"], SSD_CONV_CH, BF16)
    dt_raw = _proj(hn, p["w_dt"], 2 * SSD_N_HEADS, F32)
    q = _proj(hn, p["w_q"], ATT_WIDTH, BF16, ep=_ep_qknorm, seq_len=s, **p["qk_norm"]["q"])
    k = _proj(hn, p["w_k"], ATT_WIDTH, BF16, ep=_ep_qknorm, seq_len=s, **p["qk_norm"]["k"])
    v = _proj(hn, p["w_v"], ATT_WIDTH, BF16, seq_len=s)

    xs, bmat, cmat, dt = _conv_dt(
        xbc.reshape(b, s, SSD_CONV_CH), p["xbc_meta"], p["conv_w"], p["conv_b"],
        dt_raw.reshape(b, s, 2 * SSD_N_HEADS), p["dt_raw_meta"], p["dt_bias"].reshape(-1))
    y_ssd = _ssd(xs, bmat, cmat, dt, p["a_log"].reshape(-1), p["d_skip"],
                 p["ssd_norm_g"], z.reshape(b, s, D_MODEL))

    y_att = _attention(
        q, k, v, p["k_tail"], p["v_tail"], p["q_norm_g"], p["k_norm_g"],
        (p["lambda_q1"], p["lambda_k1"], p["lambda_q2"], p["lambda_k2"]),
        p["subln_g"], layer_idx)

    gates = _matmul([hn], [(p["w_branch_gate"], 0)], [(0, 0)], _ep_sigmoid_bias,
                    2 * D_MODEL, BF16, 1024, 1024, 52,
                    rows=[(p["b_branch_gate"].reshape(1, -1), 0)])
    bn_mix = 512
    mixed = _matmul([y_ssd.reshape(m, D_MODEL), y_att.reshape(m, ATT_WIDTH)],
                    [(p["w_ssd_out"], 0), (p["w_att_out"], 0)],
                    [(0, 0), (1, 1)], _ep_gated_merge, D_MODEL, BF16, 1024, bn_mix, 56,
                    tiles=[(gates, 0), (gates, D_MODEL // bn_mix)])
    h2, h2_bf, h2_sumsq = _matmul([mixed], [(p["w_o"], 0)], [(0, 0)], _ep_residual,
                                  D_MODEL, F32, 1024, 512, 52, tiles=[(x2, 0)], emit_stats=True)

    bn_ff = 256
    act = _matmul([h2_bf], [(p["w_gate_up"], 0), (p["w_gate_up"], D_FF // bn_ff)],
                  [(0, 0), (0, 1)], _ep_swiglu, D_FF, BF16, 2048, bn_ff, 52,
                  row_blocks=[h2_sumsq])
    out = _matmul([act], [(p["w_down"], 0)], [(0, 0)], _ep_residual,
                  D_MODEL, F32, 512, 512, 56, tiles=[(h2, 0)])
    return out.reshape(b, s, d)


def kernel(x_prompt, x_sample, meta_tokens, norm_mix_g, w_in, conv_w, conv_b, dt_bias, a_log, d_skip, ssd_norm_g, q_norm_g, k_norm_g, lambda_q1, lambda_k1, lambda_q2, lambda_k2, subln_g, w_branch_gate, b_branch_gate, w_ssd_out, w_att_out, w_o, norm_ffn_g, w_gate_up, w_down):
    names = ("norm_mix_g", "w_in", "conv_w", "conv_b", "dt_bias", "a_log", "d_skip",
             "ssd_norm_g", "q_norm_g", "k_norm_g", "lambda_q1", "lambda_k1", "lambda_q2",
             "lambda_k2", "subln_g", "w_branch_gate", "b_branch_gate", "w_ssd_out",
             "w_att_out", "w_o", "norm_ffn_g", "w_gate_up", "w_down")
    stacked = (norm_mix_g, w_in, conv_w, conv_b, dt_bias, a_log, d_skip, ssd_norm_g,
               q_norm_g, k_norm_g, lambda_q1, lambda_k1, lambda_q2, lambda_k2, subln_g,
               w_branch_gate, b_branch_gate, w_ssd_out, w_att_out, w_o, norm_ffn_g,
               w_gate_up, w_down)
    assert all(w.shape[0] == 1 for w in stacked), "single-layer encoder"
    p = _prepare(meta_tokens, {n: w[0] for n, w in zip(names, stacked)})
    return (_layer(x_prompt, 0, p), _layer(x_sample, 0, p))
```

```python
import functools
import math

import jax
import jax.numpy as jnp
from jax import lax
from jax.experimental import pallas as pl
from jax.experimental.pallas import tpu as pltpu

F32 = jnp.float32
BF16 = jnp.bfloat16

D_MODEL = 4096
N_META = 16
EPS = 1e-6
SSD_HEAD_DIM = 64
SSD_N_HEADS = D_MODEL // SSD_HEAD_DIM
SSD_N_GROUPS = 8
SSD_HEADS_PER_GROUP = SSD_N_HEADS // SSD_N_GROUPS
SSD_D_STATE = 128
SSD_CONV = 5
SSD_CHUNK = 128
SSD_GN = SSD_N_GROUPS * SSD_D_STATE
SSD_CONV_CH = D_MODEL + 2 * SSD_GN
SSD_GROUP_W = SSD_HEADS_PER_GROUP * SSD_HEAD_DIM
ATT_N_HEADS = 16
ATT_HALF_DIM = 64
ATT_V_DIM = 128
ATT_WIDTH = ATT_N_HEADS * ATT_V_DIM
D_FF = 11008
LANES = 128
HALO = 16
MIB = 1024 * 1024
LOG2E = 1.4426950408889634


def _params(n_axes, vmem_mib):
    return pltpu.CompilerParams(
        dimension_semantics=("arbitrary",) * n_axes,
        vmem_limit_bytes=vmem_mib * MIB)


def _rmsnorm_kernel(x_ref, g_ref, o_ref):
    x = x_ref[...]
    ms = jnp.mean(x * x, axis=-1, keepdims=True)
    o_ref[...] = (x * lax.rsqrt(ms + EPS) * g_ref[...]).astype(o_ref.dtype)


def _rmsnorm(x, g, bm):
    m, d = x.shape
    bm = min(bm, m)
    return pl.pallas_call(
        _rmsnorm_kernel,
        grid=(m // bm,),
        in_specs=[pl.BlockSpec((bm, d), lambda i: (i, 0)),
                  pl.BlockSpec((1, d), lambda i: (0, 0))],
        out_specs=pl.BlockSpec((bm, d), lambda i: (i, 0)),
        out_shape=jax.ShapeDtypeStruct((m, d), BF16),
        compiler_params=_params(1, 32),
    )(x, g.reshape(1, d))


def _mm_kernel(*refs, dots, n_lhs, n_rhs, n_tiles, n_rows, n_consts, epilogue, emit_stats,
               head_major):
    lhs = refs[:n_lhs]
    rhs = refs[n_lhs:n_lhs + n_rhs]
    p = n_lhs + n_rhs
    tiles = refs[p:p + n_tiles]
    p += n_tiles
    rows = refs[p:p + n_rows]
    p += n_rows
    consts = refs[p:p + n_consts]
    p += n_consts
    o_ref = refs[p]
    accs = [jnp.dot(lhs[a][...], rhs[b][...], preferred_element_type=F32)
            for a, b in dots]
    out = epilogue(accs, [t[...] for t in tiles], [r[...] for r in rows],
                   [c[...] for c in consts])
    if head_major:
        for hh in range(o_ref.shape[1]):
            o_ref[0, hh] = out[:, hh * LANES:(hh + 1) * LANES].astype(o_ref.dtype)
    else:
        o_ref[...] = out.astype(o_ref.dtype)
    if emit_stats:
        obf_ref, ss_ref = refs[p + 1], refs[p + 2]
        obf_ref[...] = out.astype(obf_ref.dtype)

        @pl.when(pl.program_id(1) == 0)
        def _():
            ss_ref[...] = jnp.zeros(ss_ref.shape, F32)

        ss_ref[...] += jnp.broadcast_to(jnp.sum(out * out, axis=1, keepdims=True), ss_ref.shape)


def _matmul(lhs, rhs, dots, epilogue, n_out, out_dtype, bm, bn, vmem_mib,
            tiles=(), rows=(), consts=(), row_blocks=(), emit_stats=False, seq_len=None):
    m = lhs[0].shape[0]
    bm = min(bm, m)
    in_specs = []
    args = []
    for a in lhs:
        in_specs.append(pl.BlockSpec((bm, a.shape[1]), lambda i, j: (i, 0)))
        args.append(a)
    for w, off in rhs:
        in_specs.append(pl.BlockSpec((w.shape[0], bn), lambda i, j, off=off: (0, j + off)))
        args.append(w)
    for t, off in tiles:
        in_specs.append(pl.BlockSpec((bm, bn), lambda i, j, off=off: (i, j + off)))
        args.append(t)
    for t in row_blocks:
        in_specs.append(pl.BlockSpec((bm, t.shape[1]), lambda i, j: (i, 0)))
        args.append(t)
    for r, off in rows:
        in_specs.append(pl.BlockSpec((1, bn), lambda i, j, off=off: (0, j + off)))
        args.append(r)
    for c in consts:
        in_specs.append(pl.BlockSpec(c.shape, lambda i, j, nd=c.ndim: (0,) * nd))
        args.append(c)
    out_specs = pl.BlockSpec((bm, bn), lambda i, j: (i, j))
    out_shape = jax.ShapeDtypeStruct((m, n_out), out_dtype)
    if seq_len is not None:
        assert seq_len % bm == 0 and bn % LANES == 0
        per_seq = seq_len // bm
        out_specs = pl.BlockSpec((1, bn // LANES, bm, LANES),
                                 lambda i, j: (i // per_seq, j, i % per_seq, 0))
        out_shape = jax.ShapeDtypeStruct((m // seq_len, n_out // LANES, seq_len, LANES), out_dtype)
    if emit_stats:
        out_specs = [out_specs, pl.BlockSpec((bm, bn), lambda i, j: (i, j)),
                     pl.BlockSpec((bm, LANES), lambda i, j: (i, 0))]
        out_shape = [out_shape, jax.ShapeDtypeStruct((m, n_out), BF16),
                     jax.ShapeDtypeStruct((m, LANES), F32)]
    kern = functools.partial(
        _mm_kernel, dots=tuple(dots), n_lhs=len(lhs), n_rhs=len(rhs),
        n_tiles=len(tiles) + len(row_blocks), n_rows=len(rows), n_consts=len(consts),
        epilogue=epilogue, emit_stats=emit_stats, head_major=seq_len is not None)
    return pl.pallas_call(
        kern,
        grid=(m // bm, n_out // bn),
        in_specs=in_specs,
        out_specs=out_specs,
        out_shape=out_shape,
        compiler_params=_params(2, vmem_mib),
    )(*args)


def _ep_plain(accs, tiles, rows, consts):
    return accs[0]


def _ep_qknorm(accs, tiles, rows, consts):
    acc = accs[0]
    ones_bd = consts[0]
    width = ones_bd.shape[0]
    outs = []
    for c in range(acc.shape[1] // width):
        a = acc[:, c * width:(c + 1) * width]
        ss = jnp.dot((a * a).astype(BF16), ones_bd, preferred_element_type=F32)
        outs.append(a * lax.rsqrt(ss * (1.0 / ATT_HALF_DIM) + EPS))
    return jnp.concatenate(outs, axis=1) * rows[0]


def _ep_sigmoid_bias(accs, tiles, rows, consts):
    return 0.5 * jnp.tanh(accs[0] + rows[0]) + 0.5


def _ep_gated_merge(accs, tiles, rows, consts):
    return tiles[0].astype(F32) * accs[0] + tiles[1].astype(F32) * accs[1]


def _ep_residual(accs, tiles, rows, consts):
    return tiles[0] + accs[0]


def _ep_swiglu(accs, tiles, rows, consts):
    inv_rms = lax.rsqrt(tiles[0][:, 0:1] * (1.0 / D_MODEL) + EPS)
    half_g = accs[0] * inv_rms
    return (half_g * jnp.tanh(half_g) + half_g) * (accs[1] * inv_rms)


def _conv_kernel(cur_ref, prev_ref, next_ref, meta_ref, w_ref, b_ref, shift_ref,
                 dtr_ref, dtm_ref, dtb_ref,
                 xs_ref, bm_ref, cm_ref, dt_ref, ext_ref, *, n_chunks):
    c = pl.program_id(1)
    pad = SSD_CHUNK - N_META
    ch = ext_ref.shape[1]
    dtype = ext_ref.dtype

    @pl.when(c == 0)
    def _():
        ext_ref[0:HALO + pad, :] = jnp.zeros((HALO + pad, ch), dtype)
        ext_ref[HALO + pad:HALO + SSD_CHUNK, :] = meta_ref[...]

    @pl.when(c == 1)
    def _():
        ext_ref[0:HALO, :] = meta_ref[...]

    @pl.when(c > 1)
    def _():
        ext_ref[0:HALO, :] = prev_ref[0]

    @pl.when(c > 0)
    def _():
        ext_ref[HALO:HALO + SSD_CHUNK, :] = cur_ref[0]

    @pl.when(c < n_chunks - 1)
    def _():
        ext_ref[HALO + SSD_CHUNK:, :] = next_ref[0]

    @pl.when(c == n_chunks - 1)
    def _():
        ext_ref[HALO + SSD_CHUNK:, :] = jnp.zeros((HALO, ch), dtype)

    half = (SSD_CONV - 1) // 2

    def conv_silu(lo, hi):
        ext = ext_ref[:, lo:hi]
        shifted = jnp.dot(shift_ref[...], ext, preferred_element_type=F32)
        acc = None
        idx = 0
        for k in range(SSD_CONV):
            if k == half:
                x = ext[HALO:HALO + SSD_CHUNK].astype(F32)
            else:
                x = shifted[idx * SSD_CHUNK:(idx + 1) * SSD_CHUNK]
                idx += 1
            term = x * w_ref[k:k + 1, lo:hi]
            acc = term if acc is None else acc + term
        half_acc = acc + b_ref[:, lo:hi]
        return half_acc * jnp.tanh(half_acc) + half_acc

    blk = 1024
    for s in range(D_MODEL // blk):
        xs_ref[0, :, s * blk:(s + 1) * blk] = conv_silu(s * blk, (s + 1) * blk).astype(xs_ref.dtype)
    bm_ref[0] = conv_silu(D_MODEL, D_MODEL + SSD_GN).astype(bm_ref.dtype)
    cm_ref[0] = conv_silu(D_MODEL + SSD_GN, SSD_CONV_CH).astype(cm_ref.dtype)

    @pl.when(c == 0)
    def _():
        xs_ref[0, 0:pad, :] = jnp.zeros((pad, D_MODEL), xs_ref.dtype)
        bm_ref[0, 0:pad, :] = jnp.zeros((pad, SSD_GN), bm_ref.dtype)
        cm_ref[0, 0:pad, :] = jnp.zeros((pad, SSD_GN), cm_ref.dtype)

    row = lax.broadcasted_iota(jnp.int32, (SSD_CHUNK, dt_ref.shape[2]), 0)
    valid = jnp.logical_or(c > 0, row >= pad)

    raw_real = dtr_ref[0]
    raw_meta = jnp.concatenate(
        [jnp.zeros((pad, raw_real.shape[1]), F32), dtm_ref[...]], axis=0)
    raw = jnp.where(c > 0, raw_real, raw_meta) + dtb_ref[...]
    sp = jnp.maximum(raw, 0.0) + jnp.log1p(jnp.exp(-jnp.abs(raw)))
    dt_ref[0] = jnp.where(valid, sp, 0.0)


def _conv_dt(xbc, xbc_meta, conv_w, conv_b, dt_raw, dt_raw_meta, dt_bias):
    b, s, ch = xbc.shape
    n_chunks = s // SSD_CHUNK + 1
    lp = n_chunks * SSD_CHUNK
    hb = SSD_CHUNK // HALO
    n_halo = s // HALO
    nd = dt_raw.shape[-1]
    kern = functools.partial(_conv_kernel, n_chunks=n_chunks)
    n_ext = 2 * HALO + SSD_CHUNK
    half = (SSD_CONV - 1) // 2
    taps = jnp.array([k - half for k in range(SSD_CONV) if k != half], jnp.int32)
    src = HALO + jnp.arange(SSD_CHUNK, dtype=jnp.int32)[None, :] + taps[:, None]
    shift = (src.reshape(-1, 1) == jnp.arange(n_ext, dtype=jnp.int32)[None, :]).astype(BF16)
    return pl.pallas_call(
        kern,
        grid=(b, n_chunks),
        in_specs=[
            pl.BlockSpec((1, SSD_CHUNK, ch), lambda i, c: (i, jnp.maximum(c - 1, 0), 0)),
            pl.BlockSpec((1, HALO, ch), lambda i, c: (i, jnp.maximum((c - 1) * hb - 1, 0), 0)),
            pl.BlockSpec((1, HALO, ch), lambda i, c: (i, jnp.minimum(c * hb, n_halo - 1), 0)),
            pl.BlockSpec((N_META, ch), lambda i, c: (0, 0)),
            pl.BlockSpec((SSD_CONV, ch), lambda i, c: (0, 0)),
            pl.BlockSpec((1, ch), lambda i, c: (0, 0)),
            pl.BlockSpec(((SSD_CONV - 1) * SSD_CHUNK, n_ext), lambda i, c: (0, 0)),
            pl.BlockSpec((1, SSD_CHUNK, nd), lambda i, c: (i, jnp.maximum(c - 1, 0), 0)),
            pl.BlockSpec((N_META, nd), lambda i, c: (0, 0)),
            pl.BlockSpec((1, nd), lambda i, c: (0, 0)),
        ],
        out_specs=[
            pl.BlockSpec((1, SSD_CHUNK, D_MODEL), lambda i, c: (i, c, 0)),
            pl.BlockSpec((1, SSD_CHUNK, SSD_GN), lambda i, c: (i, c, 0)),
            pl.BlockSpec((1, SSD_CHUNK, SSD_GN), lambda i, c: (i, c, 0)),
            pl.BlockSpec((1, SSD_CHUNK, nd), lambda i, c: (i, c, 0)),
        ],
        out_shape=[
            jax.ShapeDtypeStruct((b, lp, D_MODEL), BF16),
            jax.ShapeDtypeStruct((b, lp, SSD_GN), BF16),
            jax.ShapeDtypeStruct((b, lp, SSD_GN), BF16),
            jax.ShapeDtypeStruct((b, lp, nd), F32),
        ],
        scratch_shapes=[pltpu.VMEM((n_ext, ch), BF16)],
        compiler_params=_params(2, 40),
    )(xbc, xbc, xbc, xbc_meta, 0.5 * conv_w, 0.5 * conv_b.reshape(1, ch), shift,
      dt_raw, dt_raw_meta, dt_bias.reshape(1, nd))


def _split3(x):
    hi = x.astype(BF16)
    r = x - hi.astype(F32)
    mid = r.astype(BF16)
    lo = (r - mid.astype(F32)).astype(BF16)
    return hi, mid, lo


def _ssd_body(x_ref, b_ref, c_ref, dt_ref, alog_ref, exp_ref, state_ref, *, reverse):
    t = SSD_CHUNK
    lane0 = SSD_N_HEADS if reverse else 0
    dt = dt_ref[0]
    a = dt * (-jnp.exp(alog_ref[...]) * LOG2E)
    ri = lax.broadcasted_iota(jnp.int32, (t, t), 0)
    ci = lax.broadcasted_iota(jnp.int32, (t, t), 1)
    causal = (ci >= ri) if reverse else (ci <= ri)
    tri = causal.astype(BF16)
    a3 = _split3(a)
    cs = sum(jnp.dot(tri, p, preferred_element_type=F32) for p in a3)
    cs_t = cs.T
    dt_t = dt.T
    edge = 0 if reverse else t - 1
    cs_edge = cs[edge:edge + 1, :]
    state_in = dt * jnp.exp2(cs_edge - cs)
    edge_parts = jnp.concatenate(
        [p.astype(F32) for p in _split3(jnp.exp2(cs_edge))]
        + [jnp.zeros((HALO - 3, LANES), F32)], axis=0)
    stacked = jnp.concatenate([state_in, edge_parts], axis=0).astype(BF16)
    expanded = jnp.dot(stacked, exp_ref[...], preferred_element_type=F32)
    chunk_decay_all = expanded[t:t + 1] + expanded[t + 1:t + 2] + expanded[t + 2:t + 3]
    first_half = lax.broadcasted_iota(jnp.int32, (t, LANES), 1) < SSD_HEAD_DIM
    outs = []
    for g in range(SSD_N_GROUPS):
        gs = slice(g * SSD_GROUP_W, (g + 1) * SSD_GROUP_W)
        ns = slice(g * SSD_D_STATE, (g + 1) * SSD_D_STATE)
        bg = b_ref[0, :, ns]
        cg = c_ref[0, :, ns]
        cb = lax.dot_general(cg, bg, (((1,), (1,)), ((), ())), preferred_element_type=F32)
        cbm = jnp.where(causal, cb, 0.0)
        xg = x_ref[0, :, gs]
        x_in = (xg.astype(F32) * expanded[0:t, gs]).astype(BF16)
        new_state = lax.dot_general(bg, x_in, (((0,), (0,)), ((), ())),
                                    preferred_element_type=F32)
        prev = state_ref[g]
        y_off = jnp.dot(cg, prev.astype(BF16), preferred_element_type=F32)
        state_ref[g] = prev * chunk_decay_all[:, gs] + new_state
        for jp in range(SSD_HEADS_PER_GROUP // 2):
            xp = xg[:, jp * LANES:(jp + 1) * LANES]
            zero = jnp.zeros_like(xp)
            rhs = jnp.concatenate([jnp.where(first_half, xp, zero),
                                   jnp.where(first_half, zero, xp)], axis=0)
            mats, ecols = [], []
            for e in range(2):
                hl = lane0 + g * SSD_HEADS_PER_GROUP + 2 * jp + e
                colb = jnp.broadcast_to(cs[:, hl:hl + 1], (t, t))
                seg = jnp.minimum(colb - cs_t[hl:hl + 1, :], 0.0)
                mats.append((cbm * jnp.exp2(seg) * dt_t[hl:hl + 1, :]).astype(BF16))
                ecols.append(jnp.exp2(colb))
            y_diag = jnp.dot(jnp.concatenate(mats, axis=1), rhs, preferred_element_type=F32)
            y_pair = y_diag + (y_off[:, jp * LANES:(jp + 1) * LANES]
                               * jnp.where(first_half, ecols[0], ecols[1]))
            outs.append(y_pair)
    return outs


def _ssd_rev_kernel(x_ref, b_ref, c_ref, dt_ref, alog_ref, exp_ref, y_ref, state_ref, *, n_chunks):
    step = pl.program_id(1)

    @pl.when(step == 0)
    def _():
        state_ref[...] = jnp.zeros(state_ref.shape, F32)

    outs = _ssd_body(x_ref, b_ref, c_ref, dt_ref, alog_ref, exp_ref, state_ref, reverse=True)

    @pl.when(step < n_chunks - 1)
    def _():
        for k, y in enumerate(outs):
            y_ref[0, :, k * LANES:(k + 1) * LANES] = y


def _ssd_fwd_kernel(x_ref, b_ref, c_ref, dt_ref, alog_ref, exp_ref, yb_ref, z_ref,
                    dskip_ref, gain_ref, o_ref, state_ref, ybuf_ref):
    step = pl.program_id(1)

    @pl.when(step == 0)
    def _():
        state_ref[...] = jnp.zeros(state_ref.shape, F32)

    outs = _ssd_body(x_ref, b_ref, c_ref, dt_ref, alog_ref, exp_ref, state_ref, reverse=False)

    @pl.when(step > 0)
    def _():
        for k, y in enumerate(outs):
            ls = slice(k * LANES, (k + 1) * LANES)
            x = x_ref[0, :, ls].astype(F32)
            z = z_ref[0, :, ls].astype(F32)
            yk = (y + yb_ref[0, :, ls]) + x * dskip_ref[:, ls]
            ybuf_ref[:, ls] = yk * (z * jnp.tanh(z) + z)
        for g in range(SSD_N_GROUPS):
            gs = slice(g * SSD_GROUP_W, (g + 1) * SSD_GROUP_W)
            yg = ybuf_ref[:, gs]
            ms = jnp.mean(yg * yg, axis=-1, keepdims=True)
            o_ref[0, :, gs] = (yg * lax.rsqrt(ms + EPS) * gain_ref[:, gs]).astype(o_ref.dtype)


def _ssd(xs, bmat, cmat, dt, a_log, d_skip, ssd_norm_g, z):
    b, lp, _ = xs.shape
    n_chunks = lp // SSD_CHUNK
    s = lp - SSD_CHUNK
    nd = dt.shape[-1]
    alog = a_log.reshape(1, nd)
    heads = jnp.arange(D_MODEL, dtype=jnp.int32) // SSD_HEAD_DIM
    lanes = jnp.arange(nd, dtype=jnp.int32)
    expand_f = (lanes[:, None] == heads[None, :]).astype(BF16)
    expand_r = (lanes[:, None] == heads[None, :] + SSD_N_HEADS).astype(BF16)
    dskip = jnp.repeat(d_skip, SSD_HEAD_DIM).reshape(1, D_MODEL)
    gain = ssd_norm_g.reshape(1, D_MODEL)
    state = pltpu.VMEM((SSD_N_GROUPS, SSD_D_STATE, SSD_GROUP_W), F32)

    def chunk_specs(cmap):
        return [
            pl.BlockSpec((1, SSD_CHUNK, D_MODEL), lambda i, c: (i, cmap(c), 0)),
            pl.BlockSpec((1, SSD_CHUNK, SSD_GN), lambda i, c: (i, cmap(c), 0)),
            pl.BlockSpec((1, SSD_CHUNK, SSD_GN), lambda i, c: (i, cmap(c), 0)),
            pl.BlockSpec((1, SSD_CHUNK, nd), lambda i, c: (i, cmap(c), 0)),
            pl.BlockSpec((1, nd), lambda i, c: (0, 0)),
            pl.BlockSpec((nd, D_MODEL), lambda i, c: (0, 0)),
        ]

    rmap = lambda c: n_chunks - 1 - c
    rout = lambda c: jnp.maximum(n_chunks - 2 - c, 0)
    y_rev = pl.pallas_call(
        functools.partial(_ssd_rev_kernel, n_chunks=n_chunks),
        grid=(b, n_chunks),
        in_specs=chunk_specs(rmap),
        out_specs=pl.BlockSpec((1, SSD_CHUNK, D_MODEL), lambda i, c: (i, rout(c), 0)),
        out_shape=jax.ShapeDtypeStruct((b, s, D_MODEL), F32),
        scratch_shapes=[state],
        compiler_params=_params(2, 48),
    )(xs, bmat, cmat, dt, alog, expand_r)

    fmap = lambda c: c
    fout = lambda c: jnp.maximum(c - 1, 0)
    return pl.pallas_call(
        _ssd_fwd_kernel,
        grid=(b, n_chunks),
        in_specs=chunk_specs(fmap) + [
            pl.BlockSpec((1, SSD_CHUNK, D_MODEL), lambda i, c: (i, fout(c), 0)),
            pl.BlockSpec((1, SSD_CHUNK, D_MODEL), lambda i, c: (i, fout(c), 0)),
            pl.BlockSpec((1, D_MODEL), lambda i, c: (0, 0)),
            pl.BlockSpec((1, D_MODEL), lambda i, c: (0, 0)),
        ],
        out_specs=pl.BlockSpec((1, SSD_CHUNK, D_MODEL), lambda i, c: (i, fout(c), 0)),
        out_shape=jax.ShapeDtypeStruct((b, s, D_MODEL), BF16),
        scratch_shapes=[state, pltpu.VMEM((SSD_CHUNK, D_MODEL), F32)],
        compiler_params=_params(2, 48),
    )(xs, bmat, cmat, dt, alog, expand_f, y_rev, z, dskip, gain)


ATT_ONES_ROWS = 16
ATT_DROP_NATS = 25.0


def _attn_kernel(tiles_ref, slopes_ref, q_ref, k_ref, v_ref, kt_ref, vt_ref,
                 lq1_ref, lk1_ref, lq2_ref, lk2_ref, sub_ref, o_ref,
                 vT_ref, vtT_ref, bias_ref, s0_ref, s1_ref, acc_ref, *, tq, tk, n_kt, n_qt, lam_init):
    h = pl.program_id(1)
    i = pl.program_id(2)
    lo = tiles_ref[3 * (h * n_qt + i)]
    hi = tiles_ref[3 * (h * n_qt + i) + 1]
    use_tail = tiles_ref[3 * (h * n_qt + i) + 2]
    slope = slopes_ref[h]
    vrows = ATT_V_DIM + ATT_ONES_ROWS

    @pl.when(i == 0)
    def _():
        for t in range(n_kt):
            vT_ref[t, 0:ATT_V_DIM, :] = v_ref[0, 0, t * tk:(t + 1) * tk, :].astype(F32).T.astype(BF16)
            vT_ref[t, ATT_V_DIM:vrows, :] = jnp.ones((ATT_ONES_ROWS, tk), BF16)
        vtT_ref[0:ATT_V_DIM, :] = vt_ref[...].astype(F32).T.astype(BF16)
        vtT_ref[ATT_V_DIM:vrows, :] = jnp.ones((ATT_ONES_ROWS, LANES), BF16)
        srel = slope * (lax.broadcasted_iota(jnp.int32, (tk, tq), 0)
                        - lax.broadcasted_iota(jnp.int32, (tk, tq), 1)).astype(F32)
        bias_ref[0] = -srel
        bias_ref[1] = srel
        bias_ref[2] = jnp.abs(srel)

    q = q_ref[0, 0]
    lane = lax.broadcasted_iota(jnp.int32, q.shape, 1)
    zero = jnp.zeros_like(q)
    qs = (jnp.where(lane < ATT_HALF_DIM, q, zero), jnp.where(lane >= ATT_HALF_DIM, q, zero))

    def scores(k, bias_tile):
        return [lax.dot_general(k, qs[e], (((1,), (1,)), ((), ())),
                                preferred_element_type=F32) - bias_tile for e in range(2)]

    def probs(s, m):
        return jnp.exp2(s - m).astype(BF16)

    q0 = (i * tq).astype(F32)
    n_tail = kt_ref.shape[0]

    def with_tail():
        tail_valid = lax.broadcasted_iota(jnp.int32, (n_tail, 1), 0) < N_META
        c_tail = slope * (q0 + N_META)
        ms = []
        for e, s in enumerate(scores(kt_ref[...], bias_ref[0, 0:n_tail, :])):
            s = jnp.where(tail_valid, s, -jnp.inf)
            m = jnp.max(s, axis=0, keepdims=True)
            acc_ref[e] = jnp.dot(vtT_ref[...], probs(s, m), preferred_element_type=F32)
            ms.append(m - c_tail)
        return tuple(ms)

    def without_tail():
        acc_ref[...] = jnp.zeros(acc_ref.shape, F32)
        return (jnp.full((1, tq), -jnp.inf, F32),) * 2

    ms = lax.cond(use_tail == 1, with_tail, without_tail)

    n_half = tq // 256

    def stage_a(t, dst_ref):
        start = pl.multiple_of(t * tk, tk)
        side = jnp.where(t < i, 0, jnp.where(t > i, 1, 2))
        c = slope * jnp.abs(start.astype(F32) - q0)
        k = k_ref[0, 0, pl.ds(start, tk), :]
        cmax = [[], []]
        for hq in range(n_half):
            qsl = slice(hq * 256, (hq + 1) * 256)
            bias_tile = bias_ref[side, :, qsl]
            for e in range(2):
                s = lax.dot_general(k, qs[e][qsl], (((1,), (1,)), ((), ())),
                                    preferred_element_type=F32) - bias_tile
                dst_ref[e, :, qsl] = s
                cmax[e].append(jnp.max(s, axis=0, keepdims=True))
        return tuple(jnp.concatenate(cmax[e], axis=1) - c for e in range(2)) + (c,)

    def stage_b(t, src_ref, ms, cmax):
        vt_aug = vT_ref[t]
        out = []
        for e in range(2):
            m_new = jnp.maximum(ms[e], cmax[e])
            alpha = jnp.exp2(ms[e] - m_new)
            shift = m_new + cmax[2]
            for hq in range(n_half):
                qsl = slice(hq * 256, (hq + 1) * 256)
                p = probs(src_ref[e, :, qsl], shift[:, qsl])
                acc_ref[e, :, qsl] = (alpha[:, qsl] * acc_ref[e, :, qsl]
                                      + jnp.dot(vt_aug, p, preferred_element_type=F32))
            out.append(m_new)
        return tuple(out)

    def two_trips(u, carry):
        ms, cmax0 = carry
        t = lo + 2 * u
        cmax1 = stage_a(t + 1, s1_ref)
        ms = stage_b(t, s0_ref, ms, cmax0)
        cmax0 = stage_a(t + 2, s0_ref)
        ms = stage_b(t + 1, s1_ref, ms, cmax1)
        return ms, cmax0

    n_tiles = hi - lo
    n_two = (n_tiles - 1) // 2
    ms, cmax0 = lax.fori_loop(0, n_two, two_trips, (ms, stage_a(lo, s0_ref)))
    t_last = lo + 2 * n_two

    def finish_two(ms, cmax0):
        cmax1 = stage_a(t_last + 1, s1_ref)
        stage_b(t_last + 1, s1_ref, stage_b(t_last, s0_ref, ms, cmax0), cmax1)

    def finish_one(ms, cmax0):
        stage_b(t_last, s0_ref, ms, cmax0)

    lax.cond(n_tiles - 2 * n_two == 2, finish_two, finish_one, ms, cmax0)
    accs = (acc_ref[0], acc_ref[1])

    lam = (jnp.exp(jnp.sum(lq1_ref[...] * lk1_ref[...], axis=-1, keepdims=True))
           - jnp.exp(jnp.sum(lq2_ref[...] * lk2_ref[...], axis=-1, keepdims=True))
           + lam_init)
    o1, o2 = (a[0:ATT_V_DIM] / a[ATT_V_DIM:ATT_V_DIM + 1] for a in accs)
    o = o1 - lam * o2
    msq = jnp.mean(o * o, axis=0, keepdims=True)
    o = (o * lax.rsqrt(msq + EPS) * sub_ref[...]) * (1.0 - lam_init)
    o_ref[0] = o.T.astype(o_ref.dtype)


def _attention(q, k, v, k_tail, v_tail, q_gain, k_gain, lambdas, subln_g, layer_idx):
    b, _, s, _ = q.shape
    tq = tk = min(512, s)
    n_qt, n_kt = s // tq, s // tk
    lam_init = 0.8 - 0.6 * math.exp(-0.3 * layer_idx)
    slopes = jnp.exp2(-8.0 * jnp.arange(1, ATT_N_HEADS + 1, dtype=F32) / ATT_N_HEADS)
    bound = 1.01 * ATT_HALF_DIM ** 0.5 * jnp.max(jnp.abs(q_gain)) * jnp.max(jnp.abs(k_gain))
    reach = jnp.ceil((2.0 * bound + ATT_DROP_NATS) / slopes)
    reach = jnp.minimum(reach, float(2 * s)).astype(jnp.int32)[:, None]
    q_first = (jnp.arange(n_qt, dtype=jnp.int32) * tq)[None, :]
    lo = jnp.maximum(q_first - reach, 0) // tk
    hi = jnp.minimum((q_first + tq - 1 + reach) // tk + 1, n_kt)
    use_tail = (q_first + 1 <= reach).astype(jnp.int32)
    tiles = jnp.stack([lo, hi, use_tail], axis=-1).reshape(-1).astype(jnp.int32)
    kern = functools.partial(_attn_kernel, tq=tq, tk=tk, n_kt=n_kt, n_qt=n_qt, lam_init=lam_init)
    vec = lambda x: x.reshape(1, -1)
    small = lambda n: pl.BlockSpec((1, n), lambda bi, h, i: (0, 0))
    vrows = ATT_V_DIM + ATT_ONES_ROWS
    return pl.pallas_call(
        kern,
        grid=(b, ATT_N_HEADS, n_qt),
        in_specs=[
            pl.BlockSpec(memory_space=pltpu.SMEM),
            pl.BlockSpec(memory_space=pltpu.SMEM),
            pl.BlockSpec((1, 1, tq, LANES), lambda bi, h, i: (bi, h, i, 0)),
            pl.BlockSpec((1, 1, s, LANES), lambda bi, h, i: (bi, h, 0, 0)),
            pl.BlockSpec((1, 1, s, LANES), lambda bi, h, i: (bi, h, 0, 0)),
            pl.BlockSpec((LANES, LANES), lambda bi, h, i: (0, h)),
            pl.BlockSpec((LANES, LANES), lambda bi, h, i: (0, h)),
            small(ATT_HALF_DIM), small(ATT_HALF_DIM), small(ATT_HALF_DIM), small(ATT_HALF_DIM),
            pl.BlockSpec((ATT_V_DIM, 1), lambda bi, h, i: (0, 0)),
        ],
        out_specs=pl.BlockSpec((1, tq, LANES), lambda bi, h, i: (bi, i, h)),
        out_shape=jax.ShapeDtypeStruct((b, s, ATT_WIDTH), BF16),
        scratch_shapes=[
            pltpu.VMEM((n_kt, vrows, tk), BF16),
            pltpu.VMEM((vrows, LANES), BF16),
            pltpu.VMEM((3, tk, tq), F32),
            pltpu.VMEM((2, tk, tq), F32),
            pltpu.VMEM((2, tk, tq), F32),
            pltpu.VMEM((2, vrows, tq), F32),
        ],
        compiler_params=_params(3, 48),
    )(tiles, slopes * LOG2E, q, k, v, k_tail, v_tail, *[vec(x) for x in lambdas],
      subln_g.reshape(-1, 1))


def _proj(h_in, w, n, dtype, ep=None, **kw):
    return _matmul([h_in], [(w, 0)], [(0, 0)], ep or _ep_plain, n, dtype, 1024, min(1024, n), 52, **kw)


def _prepare(meta_tokens, wts):
    bf = lambda w: w.astype(BF16)
    o1 = D_MODEL
    o2 = o1 + SSD_CONV_CH
    o3 = o2 + 2 * SSD_N_HEADS
    o4 = o3 + ATT_WIDTH
    o5 = o4 + ATT_WIDTH
    w_in = wts["w_in"]
    p = dict(wts)
    p["w_z"], p["w_xbc"], p["w_dt"] = bf(0.5 * w_in[:, :o1]), bf(w_in[:, o1:o2]), bf(w_in[:, o2:o3])
    p["w_q"], p["w_k"], p["w_v"] = bf(w_in[:, o3:o4]), bf(w_in[:, o4:o5]), bf(w_in[:, o5:])
    for name in ("w_ssd_out", "w_att_out", "w_o", "w_down"):
        p[name] = bf(wts[name])
    p["w_branch_gate"] = bf(0.5 * wts["w_branch_gate"])
    p["b_branch_gate"] = 0.5 * wts["b_branch_gate"]
    gate_half = jnp.where(jnp.arange(2 * D_FF) < D_FF, 0.5, 1.0)
    p["w_gate_up"] = bf(wts["norm_ffn_g"][:, None] * wts["w_gate_up"] * gate_half[None, :])
    ones_bd = (jnp.arange(256)[:, None] // ATT_HALF_DIM
               == jnp.arange(256)[None, :] // ATT_HALF_DIM).astype(BF16)
    p["qk_norm"] = dict(
        q=dict(rows=[(jnp.tile(wts["q_norm_g"] * (ATT_HALF_DIM ** -0.5 * LOG2E),
                               2 * ATT_N_HEADS).reshape(1, ATT_WIDTH), 0)], consts=[ones_bd]),
        k=dict(rows=[(jnp.tile(wts["k_norm_g"], 2 * ATT_N_HEADS).reshape(1, ATT_WIDTH), 0)],
               consts=[ones_bd]))
    hn_meta = _rmsnorm(meta_tokens, wts["norm_mix_g"], N_META)
    tail_pad = ((0, LANES - N_META), (0, 0))
    p["xbc_meta"] = _proj(hn_meta, p["w_xbc"], SSD_CONV_CH, BF16)
    p["dt_raw_meta"] = _proj(hn_meta, p["w_dt"], 2 * SSD_N_HEADS, F32)
    p["k_tail"] = jnp.pad(_proj(hn_meta, p["w_k"], ATT_WIDTH, BF16, ep=_ep_qknorm,
                                **p["qk_norm"]["k"]), tail_pad)
    p["v_tail"] = jnp.pad(_proj(hn_meta, p["w_v"], ATT_WIDTH, BF16), tail_pad)
    return p


def _layer(x, layer_idx, p):
    b, s, d = x.shape
    m = b * s
    x2 = x.reshape(m, d)
    hn = _rmsnorm(x2, p["norm_mix_g"], 256)

    z = _proj(hn, p["w_z"], D_MODEL, BF16)
    xbc = _proj(hn, p["w_xbc"], SSD_CONV_CH, BF16)
    dt_raw = _proj(hn, p["w_dt"], 2 * SSD_N_HEADS, F32)
    q = _proj(hn, p["w_q"], ATT_WIDTH, BF16, ep=_ep_qknorm, seq_len=s, **p["qk_norm"]["q"])
    k = _proj(hn, p["w_k"], ATT_WIDTH, BF16, ep=_ep_qknorm, seq_len=s, **p["qk_norm"]["k"])
    v = _proj(hn, p["w_v"], ATT_WIDTH, BF16, seq_len=s)

    xs, bmat, cmat, dt = _conv_dt(
        xbc.reshape(b, s, SSD_CONV_CH), p["xbc_meta"], p["conv_w"], p["conv_b"],
        dt_raw.reshape(b, s, 2 * SSD_N_HEADS), p["dt_raw_meta"], p["dt_bias"].reshape(-1))
    y_ssd = _ssd(xs, bmat, cmat, dt, p["a_log"].reshape(-1), p["d_skip"],
                 p["ssd_norm_g"], z.reshape(b, s, D_MODEL))

    y_att = _attention(
        q, k, v, p["k_tail"], p["v_tail"], p["q_norm_g"], p["k_norm_g"],
        (p["lambda_q1"], p["lambda_k1"], p["lambda_q2"], p["lambda_k2"]),
        p["subln_g"], layer_idx)

    gates = _matmul([hn], [(p["w_branch_gate"], 0)], [(0, 0)], _ep_sigmoid_bias,
                    2 * D_MODEL, BF16, 1024, 1024, 52,
                    rows=[(p["b_branch_gate"].reshape(1, -1), 0)])
    bn_mix = 512
    mixed = _matmul([y_ssd.reshape(m, D_MODEL), y_att.reshape(m, ATT_WIDTH)],
                    [(p["w_ssd_out"], 0), (p["w_att_out"], 0)],
                    [(0, 0), (1, 1)], _ep_gated_merge, D_MODEL, BF16, 1024, bn_mix, 56,
                    tiles=[(gates, 0), (gates, D_MODEL // bn_mix)])
    h2, h2_bf, h2_sumsq = _matmul([mixed], [(p["w_o"], 0)], [(0, 0)], _ep_residual,
                                  D_MODEL, F32, 1024, 512, 52, tiles=[(x2, 0)], emit_stats=True)

    bn_ff = 256
    act = _matmul([h2_bf], [(p["w_gate_up"], 0), (p["w_gate_up"], D_FF // bn_ff)],
                  [(0, 0), (0, 1)], _ep_swiglu, D_FF, BF16, 2048, bn_ff, 52,
                  row_blocks=[h2_sumsq])
    out = _matmul([act], [(p["w_down"], 0)], [(0, 0)], _ep_residual,
                  D_MODEL, F32, 512, 512, 56, tiles=[(h2, 0)])
    return out.reshape(b, s, d)


def kernel(x_prompt, x_sample, meta_tokens, norm_mix_g, w_in, conv_w, conv_b, dt_bias, a_log, d_skip, ssd_norm_g, q_norm_g, k_norm_g, lambda_q1, lambda_k1, lambda_q2, lambda_k2, subln_g, w_branch_gate, b_branch_gate, w_ssd_out, w_att_out, w_o, norm_ffn_g, w_gate_up, w_down):
    names = ("norm_mix_g", "w_in", "conv_w", "conv_b", "dt_bias", "a_log", "d_skip",
             "ssd_norm_g", "q_norm_g", "k_norm_g", "lambda_q1", "lambda_k1", "lambda_q2",
             "lambda_k2", "subln_g", "w_branch_gate", "b_branch_gate", "w_ssd_out",
             "w_att_out", "w_o", "norm_ffn_g", "w_gate_up", "w_down")
    stacked = (norm_mix_g, w_in, conv_w, conv_b, dt_bias, a_log, d_skip, ssd_norm_g,
               q_norm_g, k_norm_g, lambda_q1, lambda_k1, lambda_q2, lambda_k2, subln_g,
               w_branch_gate, b_branch_gate, w_ssd_out, w_att_out, w_o, norm_ffn_g,
               w_gate_up, w_down)
    assert all(w.shape[0] == 1 for w in stacked), "single-layer encoder"
    p = _prepare(meta_tokens, {n: w[0] for n, w in zip(names, stacked)})
    return (_layer(x_prompt, 0, p), _layer(x_sample, 0, p))
```

```python
import functools
import math

import jax
import jax.numpy as jnp
from jax import lax
from jax.experimental import pallas as pl
from jax.experimental.pallas import tpu as pltpu

F32 = jnp.float32
BF16 = jnp.bfloat16

D_MODEL = 4096
N_META = 16
EPS = 1e-6
SSD_HEAD_DIM = 64
SSD_N_HEADS = D_MODEL // SSD_HEAD_DIM
SSD_N_GROUPS = 8
SSD_HEADS_PER_GROUP = SSD_N_HEADS // SSD_N_GROUPS
SSD_D_STATE = 128
SSD_CONV = 5
SSD_CHUNK = 128
SSD_GN = SSD_N_GROUPS * SSD_D_STATE
SSD_CONV_CH = D_MODEL + 2 * SSD_GN
SSD_GROUP_W = SSD_HEADS_PER_GROUP * SSD_HEAD_DIM
ATT_N_HEADS = 16
ATT_HALF_DIM = 64
ATT_V_DIM = 128
ATT_WIDTH = ATT_N_HEADS * ATT_V_DIM
D_FF = 11008
LANES = 128
HALO = 16
MIB = 1024 * 1024
LOG2E = 1.4426950408889634


def _params(n_axes, vmem_mib):
    return pltpu.CompilerParams(
        dimension_semantics=("arbitrary",) * n_axes,
        vmem_limit_bytes=vmem_mib * MIB)


def _rmsnorm_kernel(x_ref, g_ref, o_ref):
    x = x_ref[...]
    ms = jnp.mean(x * x, axis=-1, keepdims=True)
    o_ref[...] = (x * lax.rsqrt(ms + EPS) * g_ref[...]).astype(o_ref.dtype)


def _rmsnorm(x, g, bm):
    m, d = x.shape
    bm = min(bm, m)
    return pl.pallas_call(
        _rmsnorm_kernel,
        grid=(m // bm,),
        in_specs=[pl.BlockSpec((bm, d), lambda i: (i, 0)),
                  pl.BlockSpec((1, d), lambda i: (0, 0))],
        out_specs=pl.BlockSpec((bm, d), lambda i: (i, 0)),
        out_shape=jax.ShapeDtypeStruct((m, d), BF16),
        compiler_params=_params(1, 32),
    )(x, g.reshape(1, d))


def _mm_kernel(*refs, dots, n_lhs, n_rhs, n_tiles, n_rows, n_consts, epilogue, emit_stats,
               head_major):
    lhs = refs[:n_lhs]
    rhs = refs[n_lhs:n_lhs + n_rhs]
    p = n_lhs + n_rhs
    tiles = refs[p:p + n_tiles]
    p += n_tiles
    rows = refs[p:p + n_rows]
    p += n_rows
    consts = refs[p:p + n_consts]
    p += n_consts
    o_ref = refs[p]
    accs = [jnp.dot(lhs[a][...], rhs[b][...], preferred_element_type=F32)
            for a, b in dots]
    out = epilogue(accs, [t[...] for t in tiles], [r[...] for r in rows],
                   [c[...] for c in consts])
    if head_major:
        for hh in range(o_ref.shape[1]):
            o_ref[0, hh] = out[:, hh * LANES:(hh + 1) * LANES].astype(o_ref.dtype)
    else:
        o_ref[...] = out.astype(o_ref.dtype)
    if emit_stats:
        obf_ref, ss_ref = refs[p + 1], refs[p + 2]
        obf_ref[...] = out.astype(obf_ref.dtype)

        @pl.when(pl.program_id(1) == 0)
        def _():
            ss_ref[...] = jnp.zeros(ss_ref.shape, F32)

        ss_ref[...] += jnp.broadcast_to(jnp.sum(out * out, axis=1, keepdims=True), ss_ref.shape)


def _matmul(lhs, rhs, dots, epilogue, n_out, out_dtype, bm, bn, vmem_mib,
            tiles=(), rows=(), consts=(), row_blocks=(), emit_stats=False, seq_len=None):
    m = lhs[0].shape[0]
    bm = min(bm, m)
    in_specs = []
    args = []
    for a in lhs:
        in_specs.append(pl.BlockSpec((bm, a.shape[1]), lambda i, j: (i, 0)))
        args.append(a)
    for w, off in rhs:
        in_specs.append(pl.BlockSpec((w.shape[0], bn), lambda i, j, off=off: (0, j + off)))
        args.append(w)
    for t, off in tiles:
        in_specs.append(pl.BlockSpec((bm, bn), lambda i, j, off=off: (i, j + off)))
        args.append(t)
    for t in row_blocks:
        in_specs.append(pl.BlockSpec((bm, t.shape[1]), lambda i, j: (i, 0)))
        args.append(t)
    for r, off in rows:
        in_specs.append(pl.BlockSpec((1, bn), lambda i, j, off=off: (0, j + off)))
        args.append(r)
    for c in consts:
        in_specs.append(pl.BlockSpec(c.shape, lambda i, j, nd=c.ndim: (0,) * nd))
        args.append(c)
    out_specs = pl.BlockSpec((bm, bn), lambda i, j: (i, j))
    out_shape = jax.ShapeDtypeStruct((m, n_out), out_dtype)
    if seq_len is not None:
        assert seq_len % bm == 0 and bn % LANES == 0
        per_seq = seq_len // bm
        out_specs = pl.BlockSpec((1, bn // LANES, bm, LANES),
                                 lambda i, j: (i // per_seq, j, i % per_seq, 0))
        out_shape = jax.ShapeDtypeStruct((m // seq_len, n_out // LANES, seq_len, LANES), out_dtype)
    if emit_stats:
        out_specs = [out_specs, pl.BlockSpec((bm, bn), lambda i, j: (i, j)),
                     pl.BlockSpec((bm, LANES), lambda i, j: (i, 0))]
        out_shape = [out_shape, jax.ShapeDtypeStruct((m, n_out), BF16),
                     jax.ShapeDtypeStruct((m, LANES), F32)]
    kern = functools.partial(
        _mm_kernel, dots=tuple(dots), n_lhs=len(lhs), n_rhs=len(rhs),
        n_tiles=len(tiles) + len(row_blocks), n_rows=len(rows), n_consts=len(consts),
        epilogue=epilogue, emit_stats=emit_stats, head_major=seq_len is not None)
    return pl.pallas_call(
        kern,
        grid=(m // bm, n_out // bn),
        in_specs=in_specs,
        out_specs=out_specs,
        out_shape=out_shape,
        compiler_params=_params(2, vmem_mib),
    )(*args)


DOWN_RING = 3


def _down_kernel(x_ref, res_ref, w_hbm, o_ref, wbuf, sem, *, nj, n_steps, bn):
    step = pl.program_id(0) * nj + pl.program_id(1)

    def fetch(s):
        col = pl.multiple_of(lax.rem(s, nj) * bn, bn)
        slot = lax.rem(s, DOWN_RING)
        return pltpu.make_async_copy(w_hbm.at[:, pl.ds(col, bn)], wbuf.at[slot], sem.at[slot])

    @pl.when(step == 0)
    def _():
        for s in range(DOWN_RING - 1):
            fetch(jnp.int32(s)).start()

    @pl.when(step + (DOWN_RING - 1) < n_steps)
    def _():
        fetch(step + (DOWN_RING - 1)).start()

    fetch(step).wait()
    acc = jnp.dot(x_ref[...], wbuf[lax.rem(step, DOWN_RING)], preferred_element_type=F32)
    o_ref[...] = res_ref[...] + acc


def _down_proj(x, w, res, bm, bn, vmem_mib):
    m, k = x.shape
    n = w.shape[1]
    ni, nj = m // bm, n // bn
    assert ni * nj >= DOWN_RING
    kern = functools.partial(_down_kernel, nj=nj, n_steps=ni * nj, bn=bn)
    return pl.pallas_call(
        kern,
        grid=(ni, nj),
        in_specs=[pl.BlockSpec((bm, k), lambda i, j: (i, 0)),
                  pl.BlockSpec((bm, bn), lambda i, j: (i, j)),
                  pl.BlockSpec(memory_space=pl.ANY)],
        out_specs=pl.BlockSpec((bm, bn), lambda i, j: (i, j)),
        out_shape=jax.ShapeDtypeStruct((m, n), F32),
        scratch_shapes=[pltpu.VMEM((DOWN_RING, k, bn), BF16),
                        pltpu.SemaphoreType.DMA((DOWN_RING,))],
        compiler_params=_params(2, vmem_mib),
    )(x, res, w)


def _ep_plain(accs, tiles, rows, consts):
    return accs[0]


def _ep_qknorm(accs, tiles, rows, consts):
    acc = accs[0]
    ones_bd = consts[0]
    width = ones_bd.shape[0]
    outs = []
    for c in range(acc.shape[1] // width):
        a = acc[:, c * width:(c + 1) * width]
        ss = jnp.dot((a * a).astype(BF16), ones_bd, preferred_element_type=F32)
        outs.append(a * lax.rsqrt(ss * (1.0 / ATT_HALF_DIM) + EPS))
    return jnp.concatenate(outs, axis=1) * rows[0]


def _ep_sigmoid_bias(accs, tiles, rows, consts):
    return 0.5 * jnp.tanh(accs[0] + rows[0]) + 0.5


def _ep_gated_merge(accs, tiles, rows, consts):
    return tiles[0].astype(F32) * accs[0] + tiles[1].astype(F32) * accs[1]


def _ep_residual(accs, tiles, rows, consts):
    return tiles[0] + accs[0]


def _ep_swiglu(accs, tiles, rows, consts):
    inv_rms = lax.rsqrt(tiles[0][:, 0:1] * (1.0 / D_MODEL) + EPS)
    half_g = accs[0] * inv_rms
    return (half_g * jnp.tanh(half_g) + half_g) * (accs[1] * inv_rms)


def _conv_kernel(cur_ref, prev_ref, next_ref, meta_ref, w_ref, b_ref, shift_ref,
                 dtr_ref, dtm_ref, dtb_ref,
                 xs_ref, bm_ref, cm_ref, dt_ref, ext_ref, *, n_chunks):
    c = pl.program_id(1)
    pad = SSD_CHUNK - N_META
    ch = ext_ref.shape[1]
    dtype = ext_ref.dtype

    @pl.when(c == 0)
    def _():
        ext_ref[0:HALO + pad, :] = jnp.zeros((HALO + pad, ch), dtype)
        ext_ref[HALO + pad:HALO + SSD_CHUNK, :] = meta_ref[...]

    @pl.when(c == 1)
    def _():
        ext_ref[0:HALO, :] = meta_ref[...]

    @pl.when(c > 1)
    def _():
        ext_ref[0:HALO, :] = prev_ref[0]

    @pl.when(c > 0)
    def _():
        ext_ref[HALO:HALO + SSD_CHUNK, :] = cur_ref[0]

    @pl.when(c < n_chunks - 1)
    def _():
        ext_ref[HALO + SSD_CHUNK:, :] = next_ref[0]

    @pl.when(c == n_chunks - 1)
    def _():
        ext_ref[HALO + SSD_CHUNK:, :] = jnp.zeros((HALO, ch), dtype)

    half = (SSD_CONV - 1) // 2

    def conv_silu(lo, hi):
        ext = ext_ref[:, lo:hi]
        shifted = jnp.dot(shift_ref[...], ext, preferred_element_type=F32)
        acc = None
        idx = 0
        for k in range(SSD_CONV):
            if k == half:
                x = ext[HALO:HALO + SSD_CHUNK].astype(F32)
            else:
                x = shifted[idx * SSD_CHUNK:(idx + 1) * SSD_CHUNK]
                idx += 1
            term = x * w_ref[k:k + 1, lo:hi]
            acc = term if acc is None else acc + term
        half_acc = acc + b_ref[:, lo:hi]
        return half_acc * jnp.tanh(half_acc) + half_acc

    blk = 1024
    for s in range(D_MODEL // blk):
        xs_ref[0, :, s * blk:(s + 1) * blk] = conv_silu(s * blk, (s + 1) * blk).astype(xs_ref.dtype)
    bm_ref[0] = conv_silu(D_MODEL, D_MODEL + SSD_GN).astype(bm_ref.dtype)
    cm_ref[0] = conv_silu(D_MODEL + SSD_GN, SSD_CONV_CH).astype(cm_ref.dtype)

    @pl.when(c == 0)
    def _():
        xs_ref[0, 0:pad, :] = jnp.zeros((pad, D_MODEL), xs_ref.dtype)
        bm_ref[0, 0:pad, :] = jnp.zeros((pad, SSD_GN), bm_ref.dtype)
        cm_ref[0, 0:pad, :] = jnp.zeros((pad, SSD_GN), cm_ref.dtype)

    row = lax.broadcasted_iota(jnp.int32, (SSD_CHUNK, dt_ref.shape[2]), 0)
    valid = jnp.logical_or(c > 0, row >= pad)

    raw_real = dtr_ref[0]
    raw_meta = jnp.concatenate(
        [jnp.zeros((pad, raw_real.shape[1]), F32), dtm_ref[...]], axis=0)
    raw = jnp.where(c > 0, raw_real, raw_meta) + dtb_ref[...]
    sp = jnp.maximum(raw, 0.0) + jnp.log1p(jnp.exp(-jnp.abs(raw)))
    dt_ref[0] = jnp.where(valid, sp, 0.0)


def _conv_dt(xbc, xbc_meta, conv_w, conv_b, dt_raw, dt_raw_meta, dt_bias):
    b, s, ch = xbc.shape
    n_chunks = s // SSD_CHUNK + 1
    lp = n_chunks * SSD_CHUNK
    hb = SSD_CHUNK // HALO
    n_halo = s // HALO
    nd = dt_raw.shape[-1]
    kern = functools.partial(_conv_kernel, n_chunks=n_chunks)
    n_ext = 2 * HALO + SSD_CHUNK
    half = (SSD_CONV - 1) // 2
    taps = jnp.array([k - half for k in range(SSD_CONV) if k != half], jnp.int32)
    src = HALO + jnp.arange(SSD_CHUNK, dtype=jnp.int32)[None, :] + taps[:, None]
    shift = (src.reshape(-1, 1) == jnp.arange(n_ext, dtype=jnp.int32)[None, :]).astype(BF16)
    return pl.pallas_call(
        kern,
        grid=(b, n_chunks),
        in_specs=[
            pl.BlockSpec((1, SSD_CHUNK, ch), lambda i, c: (i, jnp.maximum(c - 1, 0), 0)),
            pl.BlockSpec((1, HALO, ch), lambda i, c: (i, jnp.maximum((c - 1) * hb - 1, 0), 0)),
            pl.BlockSpec((1, HALO, ch), lambda i, c: (i, jnp.minimum(c * hb, n_halo - 1), 0)),
            pl.BlockSpec((N_META, ch), lambda i, c: (0, 0)),
            pl.BlockSpec((SSD_CONV, ch), lambda i, c: (0, 0)),
            pl.BlockSpec((1, ch), lambda i, c: (0, 0)),
            pl.BlockSpec(((SSD_CONV - 1) * SSD_CHUNK, n_ext), lambda i, c: (0, 0)),
            pl.BlockSpec((1, SSD_CHUNK, nd), lambda i, c: (i, jnp.maximum(c - 1, 0), 0)),
            pl.BlockSpec((N_META, nd), lambda i, c: (0, 0)),
            pl.BlockSpec((1, nd), lambda i, c: (0, 0)),
        ],
        out_specs=[
            pl.BlockSpec((1, SSD_CHUNK, D_MODEL), lambda i, c: (i, c, 0)),
            pl.BlockSpec((1, SSD_CHUNK, SSD_GN), lambda i, c: (i, c, 0)),
            pl.BlockSpec((1, SSD_CHUNK, SSD_GN), lambda i, c: (i, c, 0)),
            pl.BlockSpec((1, SSD_CHUNK, nd), lambda i, c: (i, c, 0)),
        ],
        out_shape=[
            jax.ShapeDtypeStruct((b, lp, D_MODEL), BF16),
            jax.ShapeDtypeStruct((b, lp, SSD_GN), BF16),
            jax.ShapeDtypeStruct((b, lp, SSD_GN), BF16),
            jax.ShapeDtypeStruct((b, lp, nd), F32),
        ],
        scratch_shapes=[pltpu.VMEM((n_ext, ch), BF16)],
        compiler_params=_params(2, 40),
    )(xbc, xbc, xbc, xbc_meta, 0.5 * conv_w, 0.5 * conv_b.reshape(1, ch), shift,
      dt_raw, dt_raw_meta, dt_bias.reshape(1, nd))


def _split3(x):
    hi = x.astype(BF16)
    r = x - hi.astype(F32)
    mid = r.astype(BF16)
    lo = (r - mid.astype(F32)).astype(BF16)
    return hi, mid, lo


def _ssd_body(x_ref, b_ref, c_ref, dt_ref, alog_ref, exp_ref, state_ref, *, reverse):
    t = SSD_CHUNK
    lane0 = SSD_N_HEADS if reverse else 0
    dt = dt_ref[0]
    a = dt * (-jnp.exp(alog_ref[...]) * LOG2E)
    ri = lax.broadcasted_iota(jnp.int32, (t, t), 0)
    ci = lax.broadcasted_iota(jnp.int32, (t, t), 1)
    causal = (ci >= ri) if reverse else (ci <= ri)
    tri = causal.astype(BF16)
    a3 = _split3(a)
    cs = sum(jnp.dot(tri, p, preferred_element_type=F32) for p in a3)
    cs_t = cs.T
    dt_t = dt.T
    edge = 0 if reverse else t - 1
    cs_edge = cs[edge:edge + 1, :]
    state_in = dt * jnp.exp2(cs_edge - cs)
    edge_parts = jnp.concatenate(
        [p.astype(F32) for p in _split3(jnp.exp2(cs_edge))]
        + [jnp.zeros((HALO - 3, LANES), F32)], axis=0)
    stacked = jnp.concatenate([state_in, edge_parts], axis=0).astype(BF16)
    expanded = jnp.dot(stacked, exp_ref[...], preferred_element_type=F32)
    chunk_decay_all = expanded[t:t + 1] + expanded[t + 1:t + 2] + expanded[t + 2:t + 3]
    first_half = lax.broadcasted_iota(jnp.int32, (t, LANES), 1) < SSD_HEAD_DIM
    outs = []
    for g in range(SSD_N_GROUPS):
        gs = slice(g * SSD_GROUP_W, (g + 1) * SSD_GROUP_W)
        ns = slice(g * SSD_D_STATE, (g + 1) * SSD_D_STATE)
        bg = b_ref[0, :, ns]
        cg = c_ref[0, :, ns]
        cb = lax.dot_general(cg, bg, (((1,), (1,)), ((), ())), preferred_element_type=F32)
        cbm = jnp.where(causal, cb, 0.0)
        xg = x_ref[0, :, gs]
        x_in = (xg.astype(F32) * expanded[0:t, gs]).astype(BF16)
        new_state = lax.dot_general(bg, x_in, (((0,), (0,)), ((), ())),
                                    preferred_element_type=F32)
        prev = state_ref[g]
        y_off = jnp.dot(cg, prev.astype(BF16), preferred_element_type=F32)
        state_ref[g] = prev * chunk_decay_all[:, gs] + new_state
        for jp in range(SSD_HEADS_PER_GROUP // 2):
            xp = xg[:, jp * LANES:(jp + 1) * LANES]
            zero = jnp.zeros_like(xp)
            rhs = jnp.concatenate([jnp.where(first_half, xp, zero),
                                   jnp.where(first_half, zero, xp)], axis=0)
            mats, ecols = [], []
            for e in range(2):
                hl = lane0 + g * SSD_HEADS_PER_GROUP + 2 * jp + e
                colb = jnp.broadcast_to(cs[:, hl:hl + 1], (t, t))
                seg = jnp.minimum(colb - cs_t[hl:hl + 1, :], 0.0)
                mats.append((cbm * jnp.exp2(seg) * dt_t[hl:hl + 1, :]).astype(BF16))
                ecols.append(jnp.exp2(colb))
            y_diag = jnp.dot(jnp.concatenate(mats, axis=1), rhs, preferred_element_type=F32)
            y_pair = y_diag + (y_off[:, jp * LANES:(jp + 1) * LANES]
                               * jnp.where(first_half, ecols[0], ecols[1]))
            outs.append(y_pair)
    return outs


def _ssd_rev_kernel(x_ref, b_ref, c_ref, dt_ref, alog_ref, exp_ref, y_ref, state_ref, *, n_chunks):
    step = pl.program_id(1)

    @pl.when(step == 0)
    def _():
        state_ref[...] = jnp.zeros(state_ref.shape, F32)

    outs = _ssd_body(x_ref, b_ref, c_ref, dt_ref, alog_ref, exp_ref, state_ref, reverse=True)

    @pl.when(step < n_chunks - 1)
    def _():
        for k, y in enumerate(outs):
            y_ref[0, :, k * LANES:(k + 1) * LANES] = y


def _ssd_fwd_kernel(x_ref, b_ref, c_ref, dt_ref, alog_ref, exp_ref, yb_ref, z_ref,
                    dskip_ref, gain_ref, o_ref, state_ref, ybuf_ref):
    step = pl.program_id(1)

    @pl.when(step == 0)
    def _():
        state_ref[...] = jnp.zeros(state_ref.shape, F32)

    outs = _ssd_body(x_ref, b_ref, c_ref, dt_ref, alog_ref, exp_ref, state_ref, reverse=False)

    @pl.when(step > 0)
    def _():
        for k, y in enumerate(outs):
            ls = slice(k * LANES, (k + 1) * LANES)
            x = x_ref[0, :, ls].astype(F32)
            z = z_ref[0, :, ls].astype(F32)
            yk = (y + yb_ref[0, :, ls]) + x * dskip_ref[:, ls]
            ybuf_ref[:, ls] = yk * (z * jnp.tanh(z) + z)
        for g in range(SSD_N_GROUPS):
            gs = slice(g * SSD_GROUP_W, (g + 1) * SSD_GROUP_W)
            yg = ybuf_ref[:, gs]
            ms = jnp.mean(yg * yg, axis=-1, keepdims=True)
            o_ref[0, :, gs] = (yg * lax.rsqrt(ms + EPS) * gain_ref[:, gs]).astype(o_ref.dtype)


def _ssd(xs, bmat, cmat, dt, a_log, d_skip, ssd_norm_g, z):
    b, lp, _ = xs.shape
    n_chunks = lp // SSD_CHUNK
    s = lp - SSD_CHUNK
    nd = dt.shape[-1]
    alog = a_log.reshape(1, nd)
    heads = jnp.arange(D_MODEL, dtype=jnp.int32) // SSD_HEAD_DIM
    lanes = jnp.arange(nd, dtype=jnp.int32)
    expand_f = (lanes[:, None] == heads[None, :]).astype(BF16)
    expand_r = (lanes[:, None] == heads[None, :] + SSD_N_HEADS).astype(BF16)
    dskip = jnp.repeat(d_skip, SSD_HEAD_DIM).reshape(1, D_MODEL)
    gain = ssd_norm_g.reshape(1, D_MODEL)
    state = pltpu.VMEM((SSD_N_GROUPS, SSD_D_STATE, SSD_GROUP_W), F32)

    def chunk_specs(cmap):
        return [
            pl.BlockSpec((1, SSD_CHUNK, D_MODEL), lambda i, c: (i, cmap(c), 0)),
            pl.BlockSpec((1, SSD_CHUNK, SSD_GN), lambda i, c: (i, cmap(c), 0)),
            pl.BlockSpec((1, SSD_CHUNK, SSD_GN), lambda i, c: (i, cmap(c), 0)),
            pl.BlockSpec((1, SSD_CHUNK, nd), lambda i, c: (i, cmap(c), 0)),
            pl.BlockSpec((1, nd), lambda i, c: (0, 0)),
            pl.BlockSpec((nd, D_MODEL), lambda i, c: (0, 0)),
        ]

    rmap = lambda c: n_chunks - 1 - c
    rout = lambda c: jnp.maximum(n_chunks - 2 - c, 0)
    y_rev = pl.pallas_call(
        functools.partial(_ssd_rev_kernel, n_chunks=n_chunks),
        grid=(b, n_chunks),
        in_specs=chunk_specs(rmap),
        out_specs=pl.BlockSpec((1, SSD_CHUNK, D_MODEL), lambda i, c: (i, rout(c), 0)),
        out_shape=jax.ShapeDtypeStruct((b, s, D_MODEL), F32),
        scratch_shapes=[state],
        compiler_params=_params(2, 48),
    )(xs, bmat, cmat, dt, alog, expand_r)

    fmap = lambda c: c
    fout = lambda c: jnp.maximum(c - 1, 0)
    return pl.pallas_call(
        _ssd_fwd_kernel,
        grid=(b, n_chunks),
        in_specs=chunk_specs(fmap) + [
            pl.BlockSpec((1, SSD_CHUNK, D_MODEL), lambda i, c: (i, fout(c), 0)),
            pl.BlockSpec((1, SSD_CHUNK, D_MODEL), lambda i, c: (i, fout(c), 0)),
            pl.BlockSpec((1, D_MODEL), lambda i, c: (0, 0)),
            pl.BlockSpec((1, D_MODEL), lambda i, c: (0, 0)),
        ],
        out_specs=pl.BlockSpec((1, SSD_CHUNK, D_MODEL), lambda i, c: (i, fout(c), 0)),
        out_shape=jax.ShapeDtypeStruct((b, s, D_MODEL), BF16),
        scratch_shapes=[state, pltpu.VMEM((SSD_CHUNK, D_MODEL), F32)],
        compiler_params=_params(2, 48),
    )(xs, bmat, cmat, dt, alog, expand_f, y_rev, z, dskip, gain)


ATT_ONES_ROWS = 16
ATT_DROP_NATS = 25.0


def _attn_kernel(tiles_ref, slopes_ref, q_ref, k_ref, v_ref, kt_ref, vt_ref,
                 lq1_ref, lk1_ref, lq2_ref, lk2_ref, sub_ref, o_ref,
                 vT_ref, vtT_ref, bias_ref, s0_ref, s1_ref, acc_ref, *, tq, tk, n_kt, n_qt, lam_init):
    h = pl.program_id(1)
    i = pl.program_id(2)
    lo = tiles_ref[3 * (h * n_qt + i)]
    hi = tiles_ref[3 * (h * n_qt + i) + 1]
    use_tail = tiles_ref[3 * (h * n_qt + i) + 2]
    slope = slopes_ref[h]
    vrows = ATT_V_DIM + ATT_ONES_ROWS

    @pl.when(i == 0)
    def _():
        for t in range(n_kt):
            vT_ref[t, 0:ATT_V_DIM, :] = v_ref[0, 0, t * tk:(t + 1) * tk, :].astype(F32).T.astype(BF16)
            vT_ref[t, ATT_V_DIM:vrows, :] = jnp.ones((ATT_ONES_ROWS, tk), BF16)
        vtT_ref[0:ATT_V_DIM, :] = vt_ref[...].astype(F32).T.astype(BF16)
        vtT_ref[ATT_V_DIM:vrows, :] = jnp.ones((ATT_ONES_ROWS, LANES), BF16)
        srel = slope * (lax.broadcasted_iota(jnp.int32, (tk, tq), 0)
                        - lax.broadcasted_iota(jnp.int32, (tk, tq), 1)).astype(F32)
        bias_ref[0] = -srel
        bias_ref[1] = srel
        bias_ref[2] = jnp.abs(srel)

    q = q_ref[0, 0]
    lane = lax.broadcasted_iota(jnp.int32, q.shape, 1)
    zero = jnp.zeros_like(q)
    qs = (jnp.where(lane < ATT_HALF_DIM, q, zero), jnp.where(lane >= ATT_HALF_DIM, q, zero))

    def scores(k, bias_tile):
        return [lax.dot_general(k, qs[e], (((1,), (1,)), ((), ())),
                                preferred_element_type=F32) - bias_tile for e in range(2)]

    def probs(s, m):
        return jnp.exp2(s - m).astype(BF16)

    q0 = (i * tq).astype(F32)
    n_tail = kt_ref.shape[0]

    def with_tail():
        tail_valid = lax.broadcasted_iota(jnp.int32, (n_tail, 1), 0) < N_META
        c_tail = slope * (q0 + N_META)
        ms = []
        for e, s in enumerate(scores(kt_ref[...], bias_ref[0, 0:n_tail, :])):
            s = jnp.where(tail_valid, s, -jnp.inf)
            m = jnp.max(s, axis=0, keepdims=True)
            acc_ref[e] = jnp.dot(vtT_ref[...], probs(s, m), preferred_element_type=F32)
            ms.append(m - c_tail)
        return tuple(ms)

    def without_tail():
        acc_ref[...] = jnp.zeros(acc_ref.shape, F32)
        return (jnp.full((1, tq), -jnp.inf, F32),) * 2

    ms = lax.cond(use_tail == 1, with_tail, without_tail)

    n_half = tq // 256

    def stage_a(t, dst_ref):
        start = pl.multiple_of(t * tk, tk)
        side = jnp.where(t < i, 0, jnp.where(t > i, 1, 2))
        c = slope * jnp.abs(start.astype(F32) - q0)
        k = k_ref[0, 0, pl.ds(start, tk), :]
        cmax = [[], []]
        for hq in range(n_half):
            qsl = slice(hq * 256, (hq + 1) * 256)
            bias_tile = bias_ref[side, :, qsl]
            for e in range(2):
                s = lax.dot_general(k, qs[e][qsl], (((1,), (1,)), ((), ())),
                                    preferred_element_type=F32) - bias_tile
                dst_ref[e, :, qsl] = s
                cmax[e].append(jnp.max(s, axis=0, keepdims=True))
        return tuple(jnp.concatenate(cmax[e], axis=1) - c for e in range(2)) + (c,)

    def stage_b(t, src_ref, ms, cmax):
        vt_aug = vT_ref[t]
        out = []
        for e in range(2):
            m_new = jnp.maximum(ms[e], cmax[e])
            alpha = jnp.exp2(ms[e] - m_new)
            shift = m_new + cmax[2]
            for hq in range(n_half):
                qsl = slice(hq * 256, (hq + 1) * 256)
                p = probs(src_ref[e, :, qsl], shift[:, qsl])
                acc_ref[e, :, qsl] = (alpha[:, qsl] * acc_ref[e, :, qsl]
                                      + jnp.dot(vt_aug, p, preferred_element_type=F32))
            out.append(m_new)
        return tuple(out)

    def two_trips(u, carry):
        ms, cmax0 = carry
        t = lo + 2 * u
        cmax1 = stage_a(t + 1, s1_ref)
        ms = stage_b(t, s0_ref, ms, cmax0)
        cmax0 = stage_a(t + 2, s0_ref)
        ms = stage_b(t + 1, s1_ref, ms, cmax1)
        return ms, cmax0

    n_tiles = hi - lo
    n_two = (n_tiles - 1) // 2
    ms, cmax0 = lax.fori_loop(0, n_two, two_trips, (ms, stage_a(lo, s0_ref)))
    t_last = lo + 2 * n_two

    def finish_two(ms, cmax0):
        cmax1 = stage_a(t_last + 1, s1_ref)
        stage_b(t_last + 1, s1_ref, stage_b(t_last, s0_ref, ms, cmax0), cmax1)

    def finish_one(ms, cmax0):
        stage_b(t_last, s0_ref, ms, cmax0)

    lax.cond(n_tiles - 2 * n_two == 2, finish_two, finish_one, ms, cmax0)
    accs = (acc_ref[0], acc_ref[1])

    lam = (jnp.exp(jnp.sum(lq1_ref[...] * lk1_ref[...], axis=-1, keepdims=True))
           - jnp.exp(jnp.sum(lq2_ref[...] * lk2_ref[...], axis=-1, keepdims=True))
           + lam_init)
    o1, o2 = (a[0:ATT_V_DIM] / a[ATT_V_DIM:ATT_V_DIM + 1] for a in accs)
    o = o1 - lam * o2
    msq = jnp.mean(o * o, axis=0, keepdims=True)
    o = (o * lax.rsqrt(msq + EPS) * sub_ref[...]) * (1.0 - lam_init)
    o_ref[0] = o.T.astype(o_ref.dtype)


def _attention(q, k, v, k_tail, v_tail, q_gain, k_gain, lambdas, subln_g, layer_idx):
    b, _, s, _ = q.shape
    tq = tk = min(512, s)
    n_qt, n_kt = s // tq, s // tk
    lam_init = 0.8 - 0.6 * math.exp(-0.3 * layer_idx)
    slopes = jnp.exp2(-8.0 * jnp.arange(1, ATT_N_HEADS + 1, dtype=F32) / ATT_N_HEADS)
    bound = 1.01 * ATT_HALF_DIM ** 0.5 * jnp.max(jnp.abs(q_gain)) * jnp.max(jnp.abs(k_gain))
    reach = jnp.ceil((2.0 * bound + ATT_DROP_NATS) / slopes)
    reach = jnp.minimum(reach, float(2 * s)).astype(jnp.int32)[:, None]
    q_first = (jnp.arange(n_qt, dtype=jnp.int32) * tq)[None, :]
    lo = jnp.maximum(q_first - reach, 0) // tk
    hi = jnp.minimum((q_first + tq - 1 + reach) // tk + 1, n_kt)
    use_tail = (q_first + 1 <= reach).astype(jnp.int32)
    tiles = jnp.stack([lo, hi, use_tail], axis=-1).reshape(-1).astype(jnp.int32)
    kern = functools.partial(_attn_kernel, tq=tq, tk=tk, n_kt=n_kt, n_qt=n_qt, lam_init=lam_init)
    vec = lambda x: x.reshape(1, -1)
    small = lambda n: pl.BlockSpec((1, n), lambda bi, h, i: (0, 0))
    vrows = ATT_V_DIM + ATT_ONES_ROWS
    return pl.pallas_call(
        kern,
        grid=(b, ATT_N_HEADS, n_qt),
        in_specs=[
            pl.BlockSpec(memory_space=pltpu.SMEM),
            pl.BlockSpec(memory_space=pltpu.SMEM),
            pl.BlockSpec((1, 1, tq, LANES), lambda bi, h, i: (bi, h, i, 0)),
            pl.BlockSpec((1, 1, s, LANES), lambda bi, h, i: (bi, h, 0, 0)),
            pl.BlockSpec((1, 1, s, LANES), lambda bi, h, i: (bi, h, 0, 0)),
            pl.BlockSpec((LANES, LANES), lambda bi, h, i: (0, h)),
            pl.BlockSpec((LANES, LANES), lambda bi, h, i: (0, h)),
            small(ATT_HALF_DIM), small(ATT_HALF_DIM), small(ATT_HALF_DIM), small(ATT_HALF_DIM),
            pl.BlockSpec((ATT_V_DIM, 1), lambda bi, h, i: (0, 0)),
        ],
        out_specs=pl.BlockSpec((1, tq, LANES), lambda bi, h, i: (bi, i, h)),
        out_shape=jax.ShapeDtypeStruct((b, s, ATT_WIDTH), BF16),
        scratch_shapes=[
            pltpu.VMEM((n_kt, vrows, tk), BF16),
            pltpu.VMEM((vrows, LANES), BF16),
            pltpu.VMEM((3, tk, tq), F32),
            pltpu.VMEM((2, tk, tq), F32),
            pltpu.VMEM((2, tk, tq), F32),
            pltpu.VMEM((2, vrows, tq), F32),
        ],
        compiler_params=_params(3, 48),
    )(tiles, slopes * LOG2E, q, k, v, k_tail, v_tail, *[vec(x) for x in lambdas],
      subln_g.reshape(-1, 1))


def _proj(h_in, w, n, dtype, ep=None, **kw):
    return _matmul([h_in], [(w, 0)], [(0, 0)], ep or _ep_plain, n, dtype, 1024, min(1024, n), 52, **kw)


def _prepare(meta_tokens, wts):
    bf = lambda w: w.astype(BF16)
    o1 = D_MODEL
    o2 = o1 + SSD_CONV_CH
    o3 = o2 + 2 * SSD_N_HEADS
    o4 = o3 + ATT_WIDTH
    o5 = o4 + ATT_WIDTH
    w_in = wts["w_in"]
    p = dict(wts)
    p["w_z"], p["w_xbc"], p["w_dt"] = bf(0.5 * w_in[:, :o1]), bf(w_in[:, o1:o2]), bf(w_in[:, o2:o3])
    p["w_q"], p["w_k"], p["w_v"] = bf(w_in[:, o3:o4]), bf(w_in[:, o4:o5]), bf(w_in[:, o5:])
    for name in ("w_ssd_out", "w_att_out", "w_o", "w_down"):
        p[name] = bf(wts[name])
    p["w_branch_gate"] = bf(0.5 * wts["w_branch_gate"])
    p["b_branch_gate"] = 0.5 * wts["b_branch_gate"]
    gate_half = jnp.where(jnp.arange(2 * D_FF) < D_FF, 0.5, 1.0)
    p["w_gate_up"] = bf(wts["norm_ffn_g"][:, None] * wts["w_gate_up"] * gate_half[None, :])
    ones_bd = (jnp.arange(256)[:, None] // ATT_HALF_DIM
               == jnp.arange(256)[None, :] // ATT_HALF_DIM).astype(BF16)
    p["qk_norm"] = dict(
        q=dict(rows=[(jnp.tile(wts["q_norm_g"] * (ATT_HALF_DIM ** -0.5 * LOG2E),
                               2 * ATT_N_HEADS).reshape(1, ATT_WIDTH), 0)], consts=[ones_bd]),
        k=dict(rows=[(jnp.tile(wts["k_norm_g"], 2 * ATT_N_HEADS).reshape(1, ATT_WIDTH), 0)],
               consts=[ones_bd]))
    hn_meta = _rmsnorm(meta_tokens, wts["norm_mix_g"], N_META)
    tail_pad = ((0, LANES - N_META), (0, 0))
    p["xbc_meta"] = _proj(hn_meta, p["w_xbc"], SSD_CONV_CH, BF16)
    p["dt_raw_meta"] = _proj(hn_meta, p["w_dt"], 2 * SSD_N_HEADS, F32)
    p["k_tail"] = jnp.pad(_proj(hn_meta, p["w_k"], ATT_WIDTH, BF16, ep=_ep_qknorm,
                                **p["qk_norm"]["k"]), tail_pad)
    p["v_tail"] = jnp.pad(_proj(hn_meta, p["w_v"], ATT_WIDTH, BF16), tail_pad)
    return p


def _layer(x, layer_idx, p):
    b, s, d = x.shape
    m = b * s
    x2 = x.reshape(m, d)
    hn = _rmsnorm(x2, p["norm_mix_g"], 256)

    z = _proj(hn, p["w_z"], D_MODEL, BF16)
    xbc = _proj(hn, p["w_xbc"], SSD_CONV_CH, BF16)
    dt_raw = _proj(hn, p["w_dt"], 2 * SSD_N_HEADS, F32)
    q = _proj(hn, p["w_q"], ATT_WIDTH, BF16, ep=_ep_qknorm, seq_len=s, **p["qk_norm"]["q"])
    k = _proj(hn, p["w_k"], ATT_WIDTH, BF16, ep=_ep_qknorm, seq_len=s, **p["qk_norm"]["k"])
    v = _proj(hn, p["w_v"], ATT_WIDTH, BF16, seq_len=s)

    xs, bmat, cmat, dt = _conv_dt(
        xbc.reshape(b, s, SSD_CONV_CH), p["xbc_meta"], p["conv_w"], p["conv_b"],
        dt_raw.reshape(b, s, 2 * SSD_N_HEADS), p["dt_raw_meta"], p["dt_bias"].reshape(-1))
    y_ssd = _ssd(xs, bmat, cmat, dt, p["a_log"].reshape(-1), p["d_skip"],
                 p["ssd_norm_g"], z.reshape(b, s, D_MODEL))

    y_att = _attention(
        q, k, v, p["k_tail"], p["v_tail"], p["q_norm_g"], p["k_norm_g"],
        (p["lambda_q1"], p["lambda_k1"], p["lambda_q2"], p["lambda_k2"]),
        p["subln_g"], layer_idx)

    gates = _matmul([hn], [(p["w_branch_gate"], 0)], [(0, 0)], _ep_sigmoid_bias,
                    2 * D_MODEL, BF16, 1024, 1024, 52,
                    rows=[(p["b_branch_gate"].reshape(1, -1), 0)])
    bn_mix = 512
    mixed = _matmul([y_ssd.reshape(m, D_MODEL), y_att.reshape(m, ATT_WIDTH)],
                    [(p["w_ssd_out"], 0), (p["w_att_out"], 0)],
                    [(0, 0), (1, 1)], _ep_gated_merge, D_MODEL, BF16, 1024, bn_mix, 56,
                    tiles=[(gates, 0), (gates, D_MODEL // bn_mix)])
    h2, h2_bf, h2_sumsq = _matmul([mixed], [(p["w_o"], 0)], [(0, 0)], _ep_residual,
                                  D_MODEL, F32, 1024, 512, 52, tiles=[(x2, 0)], emit_stats=True)

    bn_ff = 256
    act = _matmul([h2_bf], [(p["w_gate_up"], 0), (p["w_gate_up"], D_FF // bn_ff)],
                  [(0, 0), (0, 1)], _ep_swiglu, D_FF, BF16, 2048, bn_ff, 52,
                  row_blocks=[h2_sumsq])
    out = _down_proj(act, p["w_down"], h2, 512, 512, 62)
    return out.reshape(b, s, d)


def kernel(x_prompt, x_sample, meta_tokens, norm_mix_g, w_in, conv_w, conv_b, dt_bias, a_log, d_skip, ssd_norm_g, q_norm_g, k_norm_g, lambda_q1, lambda_k1, lambda_q2, lambda_k2, subln_g, w_branch_gate, b_branch_gate, w_ssd_out, w_att_out, w_o, norm_ffn_g, w_gate_up, w_down):
    names = ("norm_mix_g", "w_in", "conv_w", "conv_b", "dt_bias", "a_log", "d_skip",
             "ssd_norm_g", "q_norm_g", "k_norm_g", "lambda_q1", "lambda_k1", "lambda_q2",
             "lambda_k2", "subln_g", "w_branch_gate", "b_branch_gate", "w_ssd_out",
             "w_att_out", "w_o", "norm_ffn_g", "w_gate_up", "w_down")
    stacked = (norm_mix_g, w_in, conv_w, conv_b, dt_bias, a_log, d_skip, ssd_norm_g,
               q_norm_g, k_norm_g, lambda_q1, lambda_k1, lambda_q2, lambda_k2, subln_g,
               w_branch_gate, b_branch_gate, w_ssd_out, w_att_out, w_o, norm_ffn_g,
               w_gate_up, w_down)
    assert all(w.shape[0] == 1 for w in stacked), "single-layer encoder"
    p = _prepare(meta_tokens, {n: w[0] for n, w in zip(names, stacked)})
    return (_layer(x_prompt, 0, p), _layer(x_sample, 0, p))
```
